```python
import jax, jax.numpy as jnp
from jax import lax
import numpy as np

D_MODEL = 1024
BATCH = 16
SEQ = 2048
DEPTH = 2
DEC_BATCH = 32
DEC_SEQ = 16
PAST_LEN = 2048

CHUNK = 64
SSD_HEADS = 8
SSD_HEAD_DIM = 64
D_SSD = SSD_HEADS * SSD_HEAD_DIM
SSD_GROUPS = 2
D_STATE = 64
SSD_CONV = 4
D_XBC = D_SSD + 2 * SSD_GROUPS * D_STATE
D_SSD_IN = D_SSD + D_XBC + SSD_HEADS
ATTN_HEADS = 8
KV_HEADS = 2
HEAD_DIM = 64
D_ATTN = ATTN_HEADS * HEAD_DIM
D_KV = KV_HEADS * HEAD_DIM
WINDOW = 128
WIN_CHUNKS = WINDOW // CHUNK
D_MIX = D_SSD + D_ATTN
D_IN = D_SSD_IN + D_ATTN + 2 * D_KV
D_FF = 2816
FFN_CONV = 3
EPS = 1e-6

kernel_name = 'hybrid_ssd_swa_streaming_encoder'


def rms_norm(x, g):
    xf = x.astype(jnp.float32)
    y = xf * lax.rsqrt(jnp.mean(xf * xf, axis=-1, keepdims=True) + EPS)
    return (y * g.astype(jnp.float32)).astype(x.dtype)


def causal_dwconv(u, prev, w, b):
    K = w.shape[0]
    L = u.shape[1]
    up = jnp.concatenate([prev.astype(u.dtype), u], axis=1)
    y = b
    for k in range(K):
        y = y + up[:, k:k + L] * w[k]
    return y, up[:, up.shape[1] - (K - 1):]


def ssd_scan(x, dt, a, bm, cm, h0):
    f32 = jnp.float32
    Bsz, L, H, P = x.shape
    N = bm.shape[-1]
    Q = min(CHUNK, L)
    nc = L // Q
    xc = x.reshape(Bsz, nc, Q, H, P).astype(f32)
    dtc = dt.reshape(Bsz, nc, Q, H).astype(f32)
    bc = bm.reshape(Bsz, nc, Q, H, N).astype(f32)
    cc = cm.reshape(Bsz, nc, Q, H, N).astype(f32)
    acs = jnp.cumsum(dtc * a, axis=2)
    seg = acs[:, :, :, None, :] - acs[:, :, None, :, :]
    causal = jnp.tril(jnp.ones((Q, Q), dtype=bool))[None, None, :, :, None]
    decay = jnp.exp(jnp.where(causal, seg, -jnp.inf))
    scores = jnp.einsum('bcthn,bcshn->bctsh', cc, bc) * decay
    y_diag = jnp.einsum('bctsh,bcsh,bcshp->bcthp', scores, dtc, xc)
    decay_end = jnp.exp(acs[:, :, -1:, :] - acs)
    chunk_states = jnp.einsum('bcsh,bcshn,bcshp->bchpn', decay_end * dtc, bc, xc)
    chunk_decay = jnp.exp(acs[:, :, -1, :])

    def step(h, inp):
        s, d = inp
        return d[:, :, None, None] * h + s, h

    h_last, h_prev = lax.scan(step, h0.astype(f32),
                              (jnp.moveaxis(chunk_states, 1, 0), jnp.moveaxis(chunk_decay, 1, 0)))
    h_prev = jnp.moveaxis(h_prev, 0, 1)
    y_off = jnp.einsum('bcthn,bchpn->bcthp', cc, h_prev) * jnp.exp(acs)[..., None]
    y = (y_diag + y_off).reshape(Bsz, L, H, P)
    return y, h_last.astype(h0.dtype)


def ssd_mixer(zxbcdt, conv_prev, h0, conv_w, conv_b, dt_bias, a_log, d_skip, norm_g):
    Bsz, L, _ = zxbcdt.shape
    z = zxbcdt[..., :D_SSD]
    xbc = zxbcdt[..., D_SSD:D_SSD + D_XBC]
    dt_raw = zxbcdt[..., D_SSD + D_XBC:]
    xbc, conv_new = causal_dwconv(xbc, conv_prev, conv_w, conv_b)
    xbc = jax.nn.silu(xbc)
    gn = SSD_GROUPS * D_STATE
    xs = xbc[..., :D_SSD].reshape(Bsz, L, SSD_HEADS, SSD_HEAD_DIM)
    rep = SSD_HEADS // SSD_GROUPS
    bm = jnp.repeat(xbc[..., D_SSD:D_SSD + gn].reshape(Bsz, L, SSD_GROUPS, D_STATE), rep, axis=2)
    cm = jnp.repeat(xbc[..., D_SSD + gn:].reshape(Bsz, L, SSD_GROUPS, D_STATE), rep, axis=2)
    dt = jax.nn.softplus(dt_raw.astype(jnp.float32) + dt_bias.astype(jnp.float32))
    a = -jnp.exp(a_log.astype(jnp.float32))
    y, h_new = ssd_scan(xs, dt, a, bm, cm, h0)
    y = y + d_skip.astype(jnp.float32)[:, None] * xs.astype(jnp.float32)
    y = y.reshape(Bsz, L, D_SSD) * jax.nn.silu(z.astype(jnp.float32))
    y = rms_norm(y, norm_g)
    return y.astype(zxbcdt.dtype), conv_new, h_new


def alibi_slopes():
    return jnp.asarray(2.0 ** (-8.0 * np.arange(1, ATTN_HEADS + 1) / ATTN_HEADS), dtype=jnp.float32)


def banded_sink_attention(q, k, v, valid, sinks):
    Bsz, N, Tq, H, D = q.shape
    Tk = k.shape[2]
    rep = H // KV_HEADS
    qg = q.reshape(Bsz, N, Tq, KV_HEADS, rep, D)
    s = jnp.einsum('bnqgrd,bnkgd->bngrqk', qg, k).astype(jnp.float32) * (D ** -0.5)
    dist = jnp.abs(jnp.arange(Tq)[:, None] + WINDOW - jnp.arange(Tk)[None, :]).astype(jnp.float32)
    s = s - alibi_slopes().reshape(KV_HEADS, rep)[:, :, None, None] * dist
    s = jnp.where(valid[None, :, None, None, None, :], s, -jnp.inf)
    sink = sinks.astype(jnp.float32).reshape(KV_HEADS, rep, 1, 1)
    m = jnp.maximum(jnp.max(s, axis=-1, keepdims=True), sink)
    p = jnp.exp(s - m)
    p = p / (jnp.sum(p, axis=-1, keepdims=True) + jnp.exp(sink - m))
    o = jnp.einsum('bngrqk,bnkgd->bnqgrd', p.astype(v.dtype), v)
    return o.reshape(Bsz, N, Tq, H * D)


def swa_prompt(q, k, v, sinks):
    Bsz, L, H, D = q.shape
    nc = L // CHUNK
    qb = q.reshape(Bsz, nc, CHUNK, H, D)
    pad = jnp.zeros((Bsz, WINDOW, KV_HEADS, D), k.dtype)
    kp = jnp.concatenate([pad, k], axis=1).reshape(Bsz, nc + WIN_CHUNKS, CHUNK, KV_HEADS, D)
    vp = jnp.concatenate([pad, v], axis=1).reshape(Bsz, nc + WIN_CHUNKS, CHUNK, KV_HEADS, D)
    kb = jnp.concatenate([kp[:, i:i + nc] for i in range(WIN_CHUNKS + 1)], axis=2)
    vb = jnp.concatenate([vp[:, i:i + nc] for i in range(WIN_CHUNKS + 1)], axis=2)
    band = (WIN_CHUNKS + 1) * CHUNK
    valid = (jnp.arange(nc)[:, None] - WIN_CHUNKS + jnp.arange(band)[None, :] // CHUNK) >= 0
    o = banded_sink_attention(qb, kb, vb, valid, sinks).reshape(Bsz, L, D_ATTN)
    return o, k[:, L - WINDOW:], v[:, L - WINDOW:]


def swa_sample(q, k, v, k_cache, v_cache, sinks):
    kb = jnp.concatenate([k_cache.astype(k.dtype), k], axis=1)
    vb = jnp.concatenate([v_cache.astype(v.dtype), v], axis=1)
    valid = jnp.ones((1, kb.shape[1]), dtype=bool)
    o = banded_sink_attention(q[:, None], kb[:, None], vb[:, None], valid, sinks)[:, 0]
    return o, kb[:, kb.shape[1] - WINDOW:], vb[:, vb.shape[1] - WINDOW:]


def trunk_layer(x, c, lw, conv_prev, h0, k_cache, v_cache, ffn_prev):
    (w_ada, b_ada, norm_mix, w_in, conv_ssd_w, conv_ssd_b, dt_bias, a_log, d_skip, ssd_norm,
     q_norm, k_norm, sinks, w_out, norm_ffn, w_up, conv_ffn_w, conv_ffn_b, w_down) = lw
    Bsz, L, _ = x.shape
    mod = (jax.nn.silu(c) @ w_ada + b_ada)[:, None, :]
    sh1, sc1, g1, sh2, sc2, g2 = jnp.split(mod, 6, axis=-1)
    h = rms_norm(x, norm_mix) * (1 + sc1) + sh1
    proj = h @ w_in
    y_ssd, conv_new, h_new = ssd_mixer(proj[..., :D_SSD_IN], conv_prev, h0, conv_ssd_w, conv_ssd_b,
                                       dt_bias, a_log, d_skip, ssd_norm)
    q = proj[..., D_SSD_IN:D_SSD_IN + D_ATTN].reshape(Bsz, L, ATTN_HEADS, HEAD_DIM)
    k = proj[..., D_SSD_IN + D_ATTN:D_SSD_IN + D_ATTN + D_KV].reshape(Bsz, L, KV_HEADS, HEAD_DIM)
    v = proj[..., D_SSD_IN + D_ATTN + D_KV:].reshape(Bsz, L, KV_HEADS, HEAD_DIM)
    q = rms_norm(q, q_norm)
    k = rms_norm(k, k_norm)
    if k_cache is None:
        o, k_new, v_new = swa_prompt(q, k, v, sinks)
    else:
        o, k_new, v_new = swa_sample(q, k, v, k_cache, v_cache, sinks)
    mix = jnp.concatenate([y_ssd, o], axis=-1) @ w_out
    x = x + g1 * mix
    h = rms_norm(x, norm_ffn) * (1 + sc2) + sh2
    up, ffn_new = causal_dwconv(h @ w_up, ffn_prev, conv_ffn_w, conv_ffn_b)
    u, gt = jnp.split(up, 2, axis=-1)
    x = x + g2 * ((jax.nn.silu(gt) * u) @ w_down)
    return x, h_new, conv_new, k_new, v_new, ffn_new


def setup_inputs(seed: int = 0) -> dict:
    key = jax.random.key(seed)
    ks = jax.random.split(key, 32)
    nrm = lambda k, shape, s: jax.random.normal(k, shape, jnp.float32) * s
    dt0 = jnp.exp(jax.random.uniform(ks[17], (DEPTH, SSD_HEADS), jnp.float32,
                                     minval=np.log(1e-3), maxval=np.log(1e-1)))
    return {
        'x_prompt': nrm(ks[0], (BATCH, SEQ, D_MODEL), 1.0),
        'x_sample': nrm(ks[1], (DEC_BATCH, DEC_SEQ, D_MODEL), 1.0),
        'c_prompt': nrm(ks[2], (BATCH, D_MODEL), 1.0),
        'c_sample': nrm(ks[3], (DEC_BATCH, D_MODEL), 1.0),
        'state_ssm': nrm(ks[4], (DEPTH, DEC_BATCH, SSD_HEADS, SSD_HEAD_DIM, D_STATE), 0.1),
        'cache_conv_ssd': nrm(ks[5], (DEPTH, DEC_BATCH, SSD_CONV - 1, D_XBC), 1.0),
        'cache_attn_k': nrm(ks[6], (DEPTH, DEC_BATCH, WINDOW, KV_HEADS, HEAD_DIM), 1.0),
        'cache_attn_v': nrm(ks[7], (DEPTH, DEC_BATCH, WINDOW, KV_HEADS, HEAD_DIM), 1.0),
        'cache_conv_ffn': nrm(ks[8], (DEPTH, DEC_BATCH, FFN_CONV - 1, 2 * D_FF), 0.5),
        'w_ada': nrm(ks[9], (DEPTH, D_MODEL, 6 * D_MODEL), 0.5 * D_MODEL ** -0.5),
        'b_ada': nrm(ks[10], (DEPTH, 6 * D_MODEL), 0.02),
        'norm_mix': 1.0 + nrm(ks[11], (DEPTH, D_MODEL), 0.02),
        'w_in': nrm(ks[12], (DEPTH, D_MODEL, D_IN), D_MODEL ** -0.5),
        'conv_ssd_w': nrm(ks[13], (DEPTH, SSD_CONV, D_XBC), SSD_CONV ** -0.5),
        'conv_ssd_b': nrm(ks[14], (DEPTH, D_XBC), 0.02),
        'dt_bias': dt0 + jnp.log(-jnp.expm1(-dt0)),
        'a_log': jnp.log(jax.random.uniform(ks[15], (DEPTH, SSD_HEADS), jnp.float32, minval=1.0, maxval=16.0)),
        'd_skip': 1.0 + nrm(ks[16], (DEPTH, SSD_HEADS), 0.1),
        'ssd_norm': 1.0 + nrm(ks[18], (DEPTH, D_SSD), 0.02),
        'q_norm': 1.0 + nrm(ks[19], (DEPTH, HEAD_DIM), 0.02),
        'k_norm': 1.0 + nrm(ks[20], (DEPTH, HEAD_DIM), 0.02),
        'sinks': nrm(ks[21], (DEPTH, ATTN_HEADS), 1.0),
        'w_out': nrm(ks[22], (DEPTH, D_MIX, D_MODEL), D_MIX ** -0.5),
        'norm_ffn': 1.0 + nrm(ks[23], (DEPTH, D_MODEL), 0.02),
        'w_up': nrm(ks[24], (DEPTH, D_MODEL, 2 * D_FF), D_MODEL ** -0.5),
        'conv_ffn_w': nrm(ks[25], (DEPTH, FFN_CONV, 2 * D_FF), FFN_CONV ** -0.5),
        'conv_ffn_b': nrm(ks[26], (DEPTH, 2 * D_FF), 0.02),
        'w_down': nrm(ks[27], (DEPTH, D_FF, D_MODEL), D_FF ** -0.5),
    }


def reference(x_prompt, x_sample, c_prompt, c_sample, state_ssm, cache_conv_ssd, cache_attn_k,
              cache_attn_v, cache_conv_ffn, w_ada, b_ada, norm_mix, w_in, conv_ssd_w, conv_ssd_b,
              dt_bias, a_log, d_skip, ssd_norm, q_norm, k_norm, sinks, w_out, norm_ffn, w_up,
              conv_ffn_w, conv_ffn_b, w_down):
    bp = x_prompt.shape[0]
    dtp = x_prompt.dtype
    conv0 = jnp.zeros((bp, SSD_CONV - 1, D_XBC), dtp)
    h00 = jnp.zeros((bp, SSD_HEADS, SSD_HEAD_DIM, D_STATE), dtp)
    ffn0 = jnp.zeros((bp, FFN_CONV - 1, 2 * D_FF), dtp)
    xp, xs = x_prompt, x_sample
    ssm_p, ssm_s, cs_p, cs_s, k_p, k_s, v_p, v_s, cf_p, cf_s = ([] for _ in range(10))
    for l in range(DEPTH):
        lw = (w_ada[l], b_ada[l], norm_mix[l], w_in[l], conv_ssd_w[l], conv_ssd_b[l], dt_bias[l],
              a_log[l], d_skip[l], ssd_norm[l], q_norm[l], k_norm[l], sinks[l], w_out[l],
              norm_ffn[l], w_up[l], conv_ffn_w[l], conv_ffn_b[l], w_down[l])
        xp, h1, c1, k1, v1, f1 = trunk_layer(xp, c_prompt, lw, conv0, h00, None, None, ffn0)
        xs, h2, c2, k2, v2, f2 = trunk_layer(xs, c_sample, lw, cache_conv_ssd[l], state_ssm[l],
                                             cache_attn_k[l], cache_attn_v[l], cache_conv_ffn[l])
        ssm_p.append(h1); cs_p.append(c1); k_p.append(k1); v_p.append(v1); cf_p.append(f1)
        ssm_s.append(h2); cs_s.append(c2); k_s.append(k2); v_s.append(v2); cf_s.append(f2)
    return (xp, xs, jnp.stack(ssm_p), jnp.stack(ssm_s), jnp.stack(cs_p), jnp.stack(cs_s),
            jnp.stack(k_p), jnp.stack(k_s), jnp.stack(v_p), jnp.stack(v_s),
            jnp.stack(cf_p), jnp.stack(cf_s))
```

```python
import functools

import numpy as np
import jax
import jax.numpy as jnp
from jax import lax
from jax.experimental import pallas as pl
from jax.experimental.pallas import tpu as pltpu

F32 = jnp.float32
BF16 = jnp.bfloat16

D_MODEL = 1024
SSD_HEADS = 8
SSD_HEAD_DIM = 64
D_SSD = 512
D_STATE = 64
D_BC = 128
SSD_CONV = 4
D_XBC = 768
ATTN_HEADS = 8
HEAD_DIM = 64
D_ATTN = 512
D_KV = 128
WINDOW = 128
CHUNK = 64
D_FF = 2816
FFN_CONV = 3
EPS = 1e-6

C_Z, C_XBC, C_DT, C_Q, C_K, C_V, C_END = 0, 512, 1280, 1792, 2304, 2432, 2560

HIST = 8
FF_BLK = 256
N_FF_BLK = D_FF // FF_BLK
TILE_T = 256
SAMPLE_SEQS = 16
VMEM_LIMIT = 56 * 1024 * 1024


def _dot(a, b):
    return jnp.dot(a, b, preferred_element_type=F32)


def _dot_nt(a, b):
    return lax.dot_general(a, b, (((1,), (1,)), ((), ())), preferred_element_type=F32)


def _split_bf16(x, parts):
    out = []
    r = x
    for i in range(parts):
        p = r.astype(BF16)
        out.append(p)
        if i + 1 < parts:
            r = r - p.astype(F32)
    return out


def _dot_exact_rhs(a_bf16, x, parts):
    acc = None
    for p in _split_bf16(x, parts):
        d = _dot(a_bf16, p)
        acc = d if acc is None else acc + d
    return acc


def _dot_exact_lhs(x, b_bf16, parts):
    acc = None
    for p in _split_bf16(x, parts):
        d = _dot(p, b_bf16)
        acc = d if acc is None else acc + d
    return acc


def _rms(x, g):
    ms = jnp.mean(x * x, axis=-1, keepdims=True)
    return x * lax.rsqrt(ms + EPS) * g


def _silu(x):
    return x * (1.0 / (1.0 + jnp.exp(-x)))


def _softplus(x):
    return jnp.maximum(x, 0.0) + jnp.log1p(jnp.exp(-jnp.abs(x)))


def _lane_lo(shape):
    return (lax.broadcasted_iota(jnp.int32, shape, len(shape) - 1) % 128) < 64


def _causal_conv(buf_ref, raw3, w_ref, b_ref, taps):
    length = raw3.shape[1]
    buf_ref[:, HIST:HIST + length, :] = raw3
    y = b_ref[...]
    for k in range(taps):
        lo = HIST - (taps - 1) + k
        y = y + buf_ref[:, lo:lo + length, :] * w_ref[k:k + 1, :]
    return y


def _ssd_segment(acs, xs, dt, bm, cm, s2, dskip, smask, q):
    pad = CHUNK - q
    lane = lax.broadcasted_iota(jnp.int32, (q, D_SSD), 1)
    row = lax.broadcasted_iota(jnp.int32, (q, D_SSD), 0)
    rowf = jnp.sum(jnp.where((lane % CHUNK) == row, acs, 0.0), axis=0, keepdims=True)
    a_end = acs[q - 1:q, :]
    xdt = xs * dt
    xw = (xdt * jnp.exp(a_end - acs)).astype(BF16)
    cm_b = cm.astype(BF16)
    y_off = _dot(cm_b, s2.astype(BF16)) * jnp.exp(acs)
    zrow = jnp.zeros((2 * D_STATE - q, D_BC), F32)
    bm_t = jnp.concatenate([bm, zrow], axis=0).T.astype(BF16)
    xw_pad = jnp.concatenate([xw, jnp.zeros((2 * D_STATE - q, D_SSD), BF16)], axis=0)
    s_new = s2 * jnp.exp(a_end) + _dot(bm_t, xw_pad) * smask

    def stack2(a, b):
        if pad:
            zero = jnp.zeros((pad, a.shape[1]), a.dtype)
            return jnp.concatenate([a, zero, b, zero], axis=0)
        return jnp.concatenate([a, b], axis=0)

    lo128 = _lane_lo((q, 128))
    bstack = stack2(bm, bm).astype(BF16)
    lane128 = lax.broadcasted_iota(jnp.int32, (q, 128), 1)
    row128 = lax.broadcasted_iota(jnp.int32, (q, 128), 0)
    causal = (lane128 % CHUNK) <= row128
    y_pairs = []
    for g in range(2):
        cg = jnp.where(lo128 if g == 0 else ~lo128, cm, 0.0).astype(BF16)
        cb = _dot_nt(cg, bstack)
        for jj in range(2):
            j = 2 * g + jj
            col = acs[:, 128 * j:128 * (j + 1)]
            seg = col - rowf[:, 128 * j:128 * (j + 1)]
            decay = jnp.exp(jnp.where(causal, seg, -jnp.inf))
            m = (cb * decay).astype(BF16)
            xp = xdt[:, 128 * j:128 * (j + 1)]
            rhs = stack2(jnp.where(lo128, xp, 0.0), jnp.where(lo128, 0.0, xp)).astype(BF16)
            y_pairs.append(_dot(m, rhs))
    y = jnp.concatenate(y_pairs, axis=1) + y_off + dskip * xs
    return y, s_new


def _attn_segment(qn, kband, vband, bias_ref, sink_ref, valid, q):
    band = kband.shape[0]
    lo_b = _lane_lo((band, 128))
    lo_q = _lane_lo((q, 128))
    k_sw = pltpu.roll(kband, 64, axis=1)
    v_sw = pltpu.roll(vband, 64, axis=1)
    kdup = (jnp.where(lo_b, kband, k_sw).astype(BF16), jnp.where(lo_b, k_sw, kband).astype(BF16))
    vdup = (jnp.where(lo_b, vband, v_sw).astype(BF16), jnp.where(lo_b, v_sw, vband).astype(BF16))
    outs = []
    for g in range(2):
        p0 = qn[:, 256 * g:256 * g + 128]
        p1 = qn[:, 256 * g + 128:256 * g + 256]
        lhs = jnp.concatenate([jnp.where(lo_q, p0, 0.0), jnp.where(lo_q, 0.0, p0),
                               jnp.where(lo_q, p1, 0.0), jnp.where(lo_q, 0.0, p1)], axis=0)
        s = _dot_nt(lhs.astype(BF16), kdup[g]) * (HEAD_DIM ** -0.5) - bias_ref[g]
        if valid is not None:
            s = jnp.where(valid, s, -jnp.inf)
        sink = sink_ref[g]
        m = jnp.maximum(jnp.max(s, axis=-1, keepdims=True), sink)
        p = jnp.exp(s - m)
        p = p / (jnp.sum(p, axis=-1, keepdims=True) + jnp.exp(sink - m))
        og = _dot(p.astype(BF16), vdup[g])
        outs.append(jnp.where(lo_q, og[0:q], og[q:2 * q]))
        outs.append(jnp.where(lo_q, og[2 * q:3 * q], og[3 * q:4 * q]))
    return jnp.concatenate(outs, axis=1)


def _state_rows(s2):
    s_t = s2.T
    half = D_SSD // 2
    return jnp.concatenate([s_t[0:half, 0:D_STATE], s_t[half:D_SSD, D_STATE:2 * D_STATE]], axis=0)


def _head_rms(x, ones_ref, g_ref):
    ssq = _dot_exact_lhs(x * x, ones_ref[...], 2)
    return x * lax.rsqrt(ssq * (1.0 / HEAD_DIM) + EPS) * g_ref[...]


def _in_proj(h, win_ref, lo, hi):
    return _dot(h, win_ref[:, lo:hi])


def _ada_kernel(c_ref, w_ref, b_ref, o_ref):
    c = c_ref[...]
    o_ref[0] = _dot(_silu(c).astype(BF16), w_ref[0].astype(BF16)) + b_ref[0]


def _ada_call(c_all, w_ada, b_ada):
    depth = w_ada.shape[0]
    n = c_all.shape[0]
    blk = 1024
    return pl.pallas_call(
        _ada_kernel,
        grid=(depth, 6 * D_MODEL // blk),
        in_specs=[
            pl.BlockSpec((n, D_MODEL), lambda l, j: (0, 0)),
            pl.BlockSpec((1, D_MODEL, blk), lambda l, j: (l, 0, j)),
            pl.BlockSpec((1, 1, blk), lambda l, j: (l, 0, j)),
        ],
        out_specs=pl.BlockSpec((1, n, blk), lambda l, j: (l, 0, j)),
        out_shape=jax.ShapeDtypeStruct((depth, n, 6 * D_MODEL), F32),
        compiler_params=pltpu.CompilerParams(
            dimension_semantics=("arbitrary", "arbitrary"), vmem_limit_bytes=VMEM_LIMIT),
        name="adaln",
    )(c_all, w_ada, b_ada.reshape(depth, 1, 6 * D_MODEL))


def _mixer_prompt_kernel(x_ref, mod_ref, nmix_ref, win_ref, cw_ref, cb_ref, dtb_ref, alog_ref,
                         dsk_ref, ssdn_ref, qn_ref, kn_ref, sink_ref, bias_ref, tril_ref,
                         onesq_ref, onesk_ref, smask_ref, wout_ref,
                         xo_ref, ssm_ref, ctail_ref, ko_ref, vo_ref,
                         xbc_buf, kbuf, vbuf, s2_ref, *, tile, n_tiles):
    t = pl.program_id(1)
    n_chunks = tile // CHUNK

    @pl.when(t == 0)
    def _():
        xbc_buf[:, 0:HIST, :] = jnp.zeros((1, HIST, D_XBC), F32)
        kbuf[0:WINDOW, :] = jnp.zeros((WINDOW, D_KV), F32)
        vbuf[0:WINDOW, :] = jnp.zeros((WINDOW, D_KV), F32)
        s2_ref[...] = jnp.zeros((2 * D_STATE, D_SSD), F32)

    x = x_ref[0]
    mod = mod_ref[0]
    sh1, sc1, g1 = mod[0:1], mod[1:2], mod[2:3]
    h = (_rms(x, nmix_ref[...]) * (1.0 + sc1) + sh1).astype(BF16)

    xbc_raw = _in_proj(h, win_ref, C_XBC, C_DT)
    conv = _causal_conv(xbc_buf, xbc_raw[None], cw_ref, cb_ref, SSD_CONV)[0]
    tail = xbc_buf[:, tile:tile + HIST, :]
    xbc_buf[:, 0:HIST, :] = tail
    xbc = _silu(conv)
    xs, bm, cm = xbc[:, 0:D_SSD], xbc[:, D_SSD:D_SSD + D_BC], xbc[:, D_SSD + D_BC:D_XBC]
    dt = _softplus(_in_proj(h, win_ref, C_DT, C_Q) + dtb_ref[...])
    acs = _dot_exact_rhs(tril_ref[...], dt * (-jnp.exp(alog_ref[...])), 3)
    smask = smask_ref[...]
    dskip = dsk_ref[...]
    ys = []
    for c in range(n_chunks):
        r = slice(c * CHUNK, (c + 1) * CHUNK)
        y_c, s_new = _ssd_segment(acs[r], xs[r], dt[r], bm[r], cm[r], s2_ref[...], dskip, smask,
                                  CHUNK)
        s2_ref[...] = s_new
        ys.append(y_c)
    z = _in_proj(h, win_ref, C_Z, C_XBC)
    y_ssd = _rms(jnp.concatenate(ys, axis=0) * _silu(z), ssdn_ref[...])

    qn = _head_rms(_in_proj(h, win_ref, C_Q, C_K), onesq_ref, qn_ref)
    kn = _head_rms(_in_proj(h, win_ref, C_K, C_V), onesk_ref, kn_ref)
    kbuf[WINDOW:WINDOW + tile, :] = kn
    vbuf[WINDOW:WINDOW + tile, :] = _in_proj(h, win_ref, C_V, C_END)
    band = WINDOW + CHUNK
    col = lax.broadcasted_iota(jnp.int32, (1, band), 1)
    os_ = []
    for c in range(n_chunks):
        first_valid = (WINDOW // CHUNK - (t * n_chunks + c)) * CHUNK
        o_c = _attn_segment(qn[c * CHUNK:(c + 1) * CHUNK],
                            kbuf[c * CHUNK:c * CHUNK + band, :],
                            vbuf[c * CHUNK:c * CHUNK + band, :],
                            bias_ref, sink_ref, col >= first_valid, CHUNK)
        os_.append(o_c)
    o = jnp.concatenate(os_, axis=0)
    k_last = kbuf[tile:tile + WINDOW, :]
    v_last = vbuf[tile:tile + WINDOW, :]
    kbuf[0:WINDOW, :] = k_last
    vbuf[0:WINDOW, :] = v_last

    mix = (_dot(y_ssd.astype(BF16), wout_ref[0:D_SSD, :])
           + _dot(o.astype(BF16), wout_ref[D_SSD:D_SSD + D_ATTN, :]))
    xo_ref[0] = x + g1 * mix

    @pl.when(t == n_tiles - 1)
    def _():
        ssm_ref[0] = _state_rows(s2_ref[...])
        ctail_ref[...] = tail
        ko_ref[0] = k_last
        vo_ref[0] = v_last


def _const_spec(shape):
    zeros = (0,) * len(shape)
    return pl.BlockSpec(shape, lambda *_: zeros, pipeline_mode=pl.Buffered(1))


def _mixer_prompt_call(x, mod, lw, consts, tile):
    b, length, _ = x.shape
    n_tiles = length // tile
    band = WINDOW + CHUNK
    kern = functools.partial(_mixer_prompt_kernel, tile=tile, n_tiles=n_tiles)
    per_b = lambda i, t: (i, 0, 0)
    return pl.pallas_call(
        kern,
        grid=(b, n_tiles),
        in_specs=[
            pl.BlockSpec((1, tile, D_MODEL), lambda i, t: (i, t, 0)),
            pl.BlockSpec((1, 8, D_MODEL), per_b),
            _const_spec((1, D_MODEL)),
            _const_spec((D_MODEL, C_END)),
            _const_spec((SSD_CONV, D_XBC)),
            _const_spec((1, D_XBC)),
            _const_spec((1, D_SSD)),
            _const_spec((1, D_SSD)),
            _const_spec((1, D_SSD)),
            _const_spec((1, D_SSD)),
            _const_spec((1, D_ATTN)),
            _const_spec((1, D_KV)),
            _const_spec((2, 4 * CHUNK, 1)),
            _const_spec((2, 4 * CHUNK, band)),
            _const_spec((tile, tile)),
            _const_spec((D_ATTN, D_ATTN)),
            _const_spec((D_KV, D_KV)),
            _const_spec((2 * D_STATE, D_SSD)),
            _const_spec((D_SSD + D_ATTN, D_MODEL)),
        ],
        out_specs=[
            pl.BlockSpec((1, tile, D_MODEL), lambda i, t: (i, t, 0)),
            pl.BlockSpec((1, D_SSD, D_STATE), per_b),
            pl.BlockSpec((1, HIST, D_XBC), per_b),
            pl.BlockSpec((1, WINDOW, D_KV), per_b),
            pl.BlockSpec((1, WINDOW, D_KV), per_b),
        ],
        out_shape=[
            jax.ShapeDtypeStruct((b, length, D_MODEL), F32),
            jax.ShapeDtypeStruct((b, D_SSD, D_STATE), F32),
            jax.ShapeDtypeStruct((b, HIST, D_XBC), F32),
            jax.ShapeDtypeStruct((b, WINDOW, D_KV), F32),
            jax.ShapeDtypeStruct((b, WINDOW, D_KV), F32),
        ],
        scratch_shapes=[
            pltpu.VMEM((1, HIST + tile, D_XBC), F32),
            pltpu.VMEM((WINDOW + tile, D_KV), F32),
            pltpu.VMEM((WINDOW + tile, D_KV), F32),
            pltpu.VMEM((2 * D_STATE, D_SSD), F32),
        ],
        compiler_params=pltpu.CompilerParams(
            dimension_semantics=("arbitrary", "arbitrary"), vmem_limit_bytes=VMEM_LIMIT),
        name="mixer_prompt",
    )(x, mod, lw["norm_mix"], lw["w_in"], lw["conv_w"], lw["conv_b"], lw["dt_bias"], lw["a_log"],
      lw["d_skip"], lw["ssd_norm"], lw["q_norm"], lw["k_norm"], consts["sink_p"], consts["bias_p"],
      consts["tril_p"], consts["ones_q"], consts["ones_k"], consts["smask"], lw["w_out"])


def _mixer_sample_kernel(x_ref, sh_ref, sc_ref, g_ref, ssm_in_ref, ccache_ref, kc_ref, vc_ref,
                         nmix_ref, win_ref, cw_ref, cb_ref, dtb_ref, alog_ref,
                         dsk_ref, ssdn_ref, qn_ref, kn_ref, sink_ref, bias_ref, tril_ref,
                         onesq_ref, onesk_ref, smask_ref, wout_ref,
                         xo_ref, ssm_ref, ctail_ref, ko_ref, vo_ref,
                         xbc_buf, xs_buf, bc_buf, dt_buf, acs_buf, q_buf, k_buf, v_buf, y_buf, o_buf,
                         *, n_seq, length):
    rows = n_seq * length
    x = x_ref[...]
    h = (_rms(x, nmix_ref[...]) * (1.0 + sc_ref[...]) + sh_ref[...]).astype(BF16)

    xbc_raw = _in_proj(h, win_ref, C_XBC, C_DT)
    xbc_buf[:, 0:HIST, :] = ccache_ref[...]
    conv = _causal_conv(xbc_buf, xbc_raw.reshape(n_seq, length, D_XBC), cw_ref, cb_ref, SSD_CONV)
    ctail_ref[...] = xbc_buf[:, length:length + HIST, :]
    xbc = _silu(conv.reshape(rows, D_XBC))
    xs_buf[...] = xbc[:, 0:D_SSD]
    bc_buf[...] = xbc[:, D_SSD:D_XBC]
    dt = _softplus(_in_proj(h, win_ref, C_DT, C_Q) + dtb_ref[...])
    dt_buf[...] = dt
    acs_buf[...] = _dot_exact_rhs(tril_ref[...], dt * (-jnp.exp(alog_ref[...])), 3)
    q_buf[...] = _head_rms(_in_proj(h, win_ref, C_Q, C_K), onesq_ref, qn_ref)
    k_buf[...] = _head_rms(_in_proj(h, win_ref, C_K, C_V), onesk_ref, kn_ref)
    v_buf[...] = _in_proj(h, win_ref, C_V, C_END)

    def body(i, carry):
        r = pl.ds(pl.multiple_of(i * length, length), length)
        bc = bc_buf[r, :]
        y_i, s_new = _ssd_segment(acs_buf[r, :], xs_buf[r, :], dt_buf[r, :], bc[:, 0:D_BC],
                                  bc[:, D_BC:2 * D_BC], ssm_in_ref[i], dsk_ref[...],
                                  smask_ref[...], length)
        ssm_ref[i] = _state_rows(s_new)
        y_buf[r, :] = y_i
        kband = jnp.concatenate([kc_ref[i], k_buf[r, :]], axis=0)
        vband = jnp.concatenate([vc_ref[i], v_buf[r, :]], axis=0)
        ko_ref[i] = kband[length:length + WINDOW]
        vo_ref[i] = vband[length:length + WINDOW]
        o_buf[r, :] = _attn_segment(q_buf[r, :], kband, vband, bias_ref, sink_ref, None, length)
        return carry

    lax.fori_loop(0, n_seq, body, 0)

    z = _in_proj(h, win_ref, C_Z, C_XBC)
    y_ssd = _rms(y_buf[...] * _silu(z), ssdn_ref[...])
    mix = (_dot(y_ssd.astype(BF16), wout_ref[0:D_SSD, :])
           + _dot(o_buf[...].astype(BF16), wout_ref[D_SSD:D_SSD + D_ATTN, :]))
    xo_ref[...] = x + g_ref[...] * mix


def _lead_spec(shape):
    tail_zeros = (0,) * (len(shape) - 1)
    return pl.BlockSpec(shape, lambda i: (i,) + tail_zeros)


def _mixer_sample_call(x2, sh, sc, g, ssm_in, ccache, kc, vc, lw, consts, n_blk, length):
    n_seq = ssm_in.shape[0]
    rows = n_blk * length
    kern = functools.partial(_mixer_sample_kernel, n_seq=n_blk, length=length)
    row_args = (x2, sh, sc, g)
    seq_args = (ssm_in, ccache, kc, vc)
    const_args = (lw["norm_mix"], lw["w_in"], lw["conv_w"], lw["conv_b"], lw["dt_bias"],
                  lw["a_log"], lw["d_skip"], lw["ssd_norm"], lw["q_norm"], lw["k_norm"],
                  consts["sink_s"], consts["bias_s"], consts["tril_s"], consts["ones_q"],
                  consts["ones_k"], consts["smask"], lw["w_out"])
    out_shape = [
        jax.ShapeDtypeStruct((n_seq * length, D_MODEL), F32),
        jax.ShapeDtypeStruct((n_seq, D_SSD, D_STATE), F32),
        jax.ShapeDtypeStruct((n_seq, HIST, D_XBC), F32),
        jax.ShapeDtypeStruct((n_seq, WINDOW, D_KV), F32),
        jax.ShapeDtypeStruct((n_seq, WINDOW, D_KV), F32),
    ]
    return pl.pallas_call(
        kern,
        grid=(n_seq // n_blk,),
        in_specs=([_lead_spec((rows, D_MODEL)) for _ in row_args]
                  + [_lead_spec((n_blk,) + a.shape[1:]) for a in seq_args]
                  + [_const_spec(a.shape) for a in const_args]),
        out_specs=([_lead_spec((rows, D_MODEL))]
                   + [_lead_spec((n_blk,) + s.shape[1:]) for s in out_shape[1:]]),
        out_shape=out_shape,
        scratch_shapes=[
            pltpu.VMEM((n_blk, HIST + length, D_XBC), F32),
            pltpu.VMEM((rows, D_SSD), F32),
            pltpu.VMEM((rows, 2 * D_BC), F32),
            pltpu.VMEM((rows, D_SSD), F32),
            pltpu.VMEM((rows, D_SSD), F32),
            pltpu.VMEM((rows, D_ATTN), F32),
            pltpu.VMEM((rows, D_KV), F32),
            pltpu.VMEM((rows, D_KV), F32),
            pltpu.VMEM((rows, D_SSD), F32),
            pltpu.VMEM((rows, D_ATTN), F32),
        ],
        compiler_params=pltpu.CompilerParams(
            dimension_semantics=("arbitrary",), vmem_limit_bytes=VMEM_LIMIT),
        name="mixer_sample",
    )(*row_args, *seq_args, *const_args)


def _ffn_body(x, sh, sc, g, hist_ref, nffn_ref, wup_ref, cw_ref, cb_ref, wdn_ref, tail_ref,
              ubuf, gbuf, n_seq, length):
    rows = n_seq * length
    h = (_rms(x, nffn_ref[...]) * (1.0 + sc) + sh).astype(BF16)
    acc = None
    for j in range(N_FF_BLK):
        act = None
        halves = []
        for buf, base in ((ubuf, 0), (gbuf, D_FF)):
            lo = base + j * FF_BLK
            raw = _dot(h, wup_ref[:, lo:lo + FF_BLK]).reshape(n_seq, length, FF_BLK)
            buf[:, 0:HIST, :] = hist_ref[:, :, lo:lo + FF_BLK]
            y = _causal_conv(buf, raw, cw_ref.at[:, lo:lo + FF_BLK], cb_ref.at[:, lo:lo + FF_BLK],
                             FFN_CONV)
            tail_ref[:, :, lo:lo + FF_BLK] = buf[:, length:length + HIST, :]
            halves.append(y.reshape(rows, FF_BLK))
        act = (_silu(halves[1]) * halves[0]).astype(BF16)
        d = _dot(act, wdn_ref[j * FF_BLK:(j + 1) * FF_BLK, :])
        acc = d if acc is None else acc + d
    return x + g * acc


def _ffn_prompt_kernel(x_ref, mod_ref, nffn_ref, wup_ref, cw_ref, cb_ref, wdn_ref,
                       xo_ref, tail_ref, hist, ubuf, gbuf, *, tile, n_tiles):
    t = pl.program_id(1)

    @pl.when(t == 0)
    def _():
        hist[...] = jnp.zeros(hist.shape, F32)

    mod = mod_ref[0]
    xo_ref[0] = _ffn_body(x_ref[0], mod[3:4], mod[4:5], mod[5:6], hist, nffn_ref, wup_ref, cw_ref,
                          cb_ref, wdn_ref, hist, ubuf, gbuf, 1, tile)

    @pl.when(t == n_tiles - 1)
    def _():
        tail_ref[...] = hist[...]


def _ffn_prompt_call(x, mod, lw, tile):
    b, length, _ = x.shape
    n_tiles = length // tile
    kern = functools.partial(_ffn_prompt_kernel, tile=tile, n_tiles=n_tiles)
    per_b = lambda i, t: (i, 0, 0)
    return pl.pallas_call(
        kern,
        grid=(b, n_tiles),
        in_specs=[
            pl.BlockSpec((1, tile, D_MODEL), lambda i, t: (i, t, 0)),
            pl.BlockSpec((1, 8, D_MODEL), per_b),
            _const_spec((1, D_MODEL)),
            _const_spec((D_MODEL, 2 * D_FF)),
            _const_spec((FFN_CONV, 2 * D_FF)),
            _const_spec((1, 2 * D_FF)),
            _const_spec((D_FF, D_MODEL)),
        ],
        out_specs=[
            pl.BlockSpec((1, tile, D_MODEL), lambda i, t: (i, t, 0)),
            pl.BlockSpec((1, HIST, 2 * D_FF), per_b),
        ],
        out_shape=[
            jax.ShapeDtypeStruct((b, length, D_MODEL), F32),
            jax.ShapeDtypeStruct((b, HIST, 2 * D_FF), F32),
        ],
        scratch_shapes=[
            pltpu.VMEM((1, HIST, 2 * D_FF), F32),
            pltpu.VMEM((1, HIST + tile, FF_BLK), F32),
            pltpu.VMEM((1, HIST + tile, FF_BLK), F32),
        ],
        compiler_params=pltpu.CompilerParams(
            dimension_semantics=("arbitrary", "arbitrary"), vmem_limit_bytes=VMEM_LIMIT),
        name="ffn_prompt",
    )(x, mod, lw["norm_ffn"], lw["w_up"], lw["conv_ffn_w"], lw["conv_ffn_b"], lw["w_down"])


def _ffn_sample_kernel(x_ref, sh_ref, sc_ref, g_ref, hist_ref, nffn_ref, wup_ref, cw_ref, cb_ref,
                       wdn_ref, xo_ref, tail_ref, ubuf, gbuf, *, n_seq, length):
    xo_ref[...] = _ffn_body(x_ref[...], sh_ref[...], sc_ref[...], g_ref[...], hist_ref, nffn_ref,
                            wup_ref, cw_ref, cb_ref, wdn_ref, tail_ref, ubuf, gbuf, n_seq, length)


def _ffn_sample_call(x2, sh, sc, g, hist, lw, n_blk, length):
    n_seq = hist.shape[0]
    rows = n_blk * length
    kern = functools.partial(_ffn_sample_kernel, n_seq=n_blk, length=length)
    row_args = (x2, sh, sc, g)
    const_args = (lw["norm_ffn"], lw["w_up"], lw["conv_ffn_w"], lw["conv_ffn_b"], lw["w_down"])
    out_shape = [
        jax.ShapeDtypeStruct((n_seq * length, D_MODEL), F32),
        jax.ShapeDtypeStruct((n_seq, HIST, 2 * D_FF), F32),
    ]
    return pl.pallas_call(
        kern,
        grid=(n_seq // n_blk,),
        in_specs=([_lead_spec((rows, D_MODEL)) for _ in row_args]
                  + [_lead_spec((n_blk, HIST, 2 * D_FF))]
                  + [_const_spec(a.shape) for a in const_args]),
        out_specs=[_lead_spec((rows, D_MODEL)), _lead_spec((n_blk, HIST, 2 * D_FF))],
        out_shape=out_shape,
        scratch_shapes=[
            pltpu.VMEM((n_blk, HIST + length, FF_BLK), F32),
            pltpu.VMEM((n_blk, HIST + length, FF_BLK), F32),
        ],
        compiler_params=pltpu.CompilerParams(
            dimension_semantics=("arbitrary",), vmem_limit_bytes=VMEM_LIMIT),
        name="ffn_sample",
    )(*row_args, hist, *const_args)


def _expand_heads(v):
    return jnp.repeat(v.astype(F32), SSD_HEAD_DIM)[None, :]


def _prep_layer(l, w_in, conv_ssd_w, conv_ssd_b, dt_bias, a_log, d_skip, ssd_norm, q_norm, k_norm,
                w_out, norm_mix, norm_ffn, w_up, conv_ffn_w, conv_ffn_b, w_down):
    wi = w_in[l]
    o_xbc, o_dt = D_SSD, D_SSD + D_XBC
    o_q = o_dt + SSD_HEADS
    o_k, o_v = o_q + D_ATTN, o_q + D_ATTN + D_KV
    w_dt = jnp.repeat(wi[:, o_dt:o_q], SSD_HEAD_DIM, axis=1)
    w_in_r = jnp.concatenate(
        [wi[:, :o_xbc], wi[:, o_xbc:o_dt], w_dt, wi[:, o_q:o_k], wi[:, o_k:o_v], wi[:, o_v:]],
        axis=1).astype(BF16)
    return {
        "norm_mix": norm_mix[l][None, :],
        "w_in": w_in_r,
        "conv_w": conv_ssd_w[l],
        "conv_b": conv_ssd_b[l][None, :],
        "dt_bias": _expand_heads(dt_bias[l]),
        "a_log": _expand_heads(a_log[l]),
        "d_skip": _expand_heads(d_skip[l]),
        "ssd_norm": ssd_norm[l][None, :],
        "q_norm": jnp.tile(q_norm[l], ATTN_HEADS)[None, :],
        "k_norm": jnp.tile(k_norm[l], D_KV // HEAD_DIM)[None, :],
        "w_out": w_out[l].astype(BF16),
        "norm_ffn": norm_ffn[l][None, :],
        "w_up": w_up[l].astype(BF16),
        "conv_ffn_w": conv_ffn_w[l],
        "conv_ffn_b": conv_ffn_b[l][None, :],
        "w_down": w_down[l].astype(BF16),
    }


def _block_tril(n, blk):
    i = np.arange(n)
    return jnp.asarray(((i[:, None] // blk) == (i[None, :] // blk)) & (i[None, :] <= i[:, None]),
                       dtype=BF16)


def _block_ones(n, blk):
    i = np.arange(n)
    return jnp.asarray((i[:, None] // blk) == (i[None, :] // blk), dtype=BF16)


def _alibi_bias(q):
    slopes = 2.0 ** (-8.0 * np.arange(1, ATTN_HEADS + 1) / ATTN_HEADS)
    dist = np.abs(np.arange(q)[:, None] + WINDOW - np.arange(WINDOW + q)[None, :]).astype(np.float32)
    bias = jnp.asarray(slopes, F32)[:, None, None] * jnp.asarray(dist, F32)[None]
    return bias.reshape(2, 4 * q, WINDOW + q)


def _sink_cols(sinks_l, q):
    return jnp.repeat(sinks_l.astype(F32), q).reshape(2, 4 * q, 1)


def _state_mask():
    r = np.arange(2 * D_STATE)[:, None] // D_STATE
    c = np.arange(D_SSD)[None, :] // (D_SSD // 2)
    return jnp.asarray(r == c, dtype=F32)


def _state_to_s2(state):
    n = state.shape[0]
    st = jnp.transpose(state, (0, 3, 1, 2)).reshape(n, D_STATE, D_SSD)
    half = D_SSD // 2
    zero = jnp.zeros((n, D_STATE, half), state.dtype)
    top = jnp.concatenate([st[:, :, :half], zero], axis=2)
    bot = jnp.concatenate([zero, st[:, :, half:]], axis=2)
    return jnp.concatenate([top, bot], axis=1)


def _pad_hist(cache):
    n, k, c = cache.shape
    return jnp.concatenate([jnp.zeros((n, HIST - k, c), cache.dtype), cache], axis=1)


def kernel(x_prompt, x_sample, c_prompt, c_sample, state_ssm, cache_conv_ssd, cache_attn_k,
           cache_attn_v, cache_conv_ffn, w_ada, b_ada, norm_mix, w_in, conv_ssd_w, conv_ssd_b,
           dt_bias, a_log, d_skip, ssd_norm, q_norm, k_norm, sinks, w_out, norm_ffn, w_up,
           conv_ffn_w, conv_ffn_b, w_down):
    depth = w_ada.shape[0]
    bp, seq, _ = x_prompt.shape
    bs, lsamp, _ = x_sample.shape
    rows_s = bs * lsamp
    tile = TILE_T
    n_blk = SAMPLE_SEQS

    mod_all = _ada_call(jnp.concatenate([c_prompt, c_sample], axis=0), w_ada, b_ada)
    mod_all = mod_all.reshape(depth, bp + bs, 6, D_MODEL)
    mod_p = jnp.pad(mod_all[:, :bp], ((0, 0), (0, 0), (0, 2), (0, 0)))
    mod_s = jnp.repeat(mod_all[:, bp:], lsamp, axis=1)

    consts = {
        "tril_p": _block_tril(tile, CHUNK),
        "tril_s": _block_tril(n_blk * lsamp, lsamp),
        "ones_q": _block_ones(D_ATTN, HEAD_DIM),
        "ones_k": _block_ones(D_KV, HEAD_DIM),
        "smask": _state_mask(),
        "bias_p": _alibi_bias(CHUNK),
        "bias_s": _alibi_bias(lsamp),
    }

    xp = x_prompt
    xs = x_sample.reshape(rows_s, D_MODEL)
    outs = [[] for _ in range(10)]
    for l in range(depth):
        lw = _prep_layer(l, w_in, conv_ssd_w, conv_ssd_b, dt_bias, a_log, d_skip, ssd_norm, q_norm,
                         k_norm, w_out, norm_mix, norm_ffn, w_up, conv_ffn_w, conv_ffn_b, w_down)
        consts["sink_p"] = _sink_cols(sinks[l], CHUNK)
        consts["sink_s"] = _sink_cols(sinks[l], lsamp)

        xp, ssm_p, ct_p, k_p, v_p = _mixer_prompt_call(xp, mod_p[l], lw, consts, tile)
        xp, ft_p = _ffn_prompt_call(xp, mod_p[l], lw, tile)

        ms = mod_s[l]
        xs, ssm_s, ct_s, k_s, v_s = _mixer_sample_call(
            xs, ms[:, 0], ms[:, 1], ms[:, 2], _state_to_s2(state_ssm[l]),
            _pad_hist(cache_conv_ssd[l]), cache_attn_k[l].reshape(bs, WINDOW, D_KV),
            cache_attn_v[l].reshape(bs, WINDOW, D_KV), lw, consts, n_blk, lsamp)
        xs, ft_s = _ffn_sample_call(xs, ms[:, 3], ms[:, 4], ms[:, 5], _pad_hist(cache_conv_ffn[l]),
                                    lw, n_blk, lsamp)

        kv_shape = (WINDOW, D_KV // HEAD_DIM, HEAD_DIM)
        for lst, val in zip(outs, (
                ssm_p.reshape(bp, SSD_HEADS, SSD_HEAD_DIM, D_STATE),
                ssm_s.reshape(bs, SSD_HEADS, SSD_HEAD_DIM, D_STATE),
                ct_p[:, HIST - (SSD_CONV - 1):], ct_s[:, HIST - (SSD_CONV - 1):],
                k_p.reshape((bp,) + kv_shape), k_s.reshape((bs,) + kv_shape),
                v_p.reshape((bp,) + kv_shape), v_s.reshape((bs,) + kv_shape),
                ft_p[:, HIST - (FFN_CONV - 1):], ft_s[:, HIST - (FFN_CONV - 1):])):
            lst.append(val)

    return (xp, xs.reshape(bs, lsamp, D_MODEL)) + tuple(jnp.stack(o) for o in outs)
```

```python
import functools

import numpy as np
import jax
import jax.numpy as jnp
from jax import lax
from jax.experimental import pallas as pl
from jax.experimental.pallas import tpu as pltpu

F32 = jnp.float32
BF16 = jnp.bfloat16

D_MODEL = 1024
SSD_HEADS = 8
SSD_HEAD_DIM = 64
D_SSD = 512
D_STATE = 64
D_BC = 128
SSD_CONV = 4
D_XBC = 768
ATTN_HEADS = 8
HEAD_DIM = 64
D_ATTN = 512
D_KV = 128
WINDOW = 128
CHUNK = 64
D_FF = 2816
FFN_CONV = 3
EPS = 1e-6

C_Z, C_XBC, C_DT, C_Q, C_K, C_V, C_END = 0, 512, 1280, 1792, 2304, 2432, 2560

HIST = 8
FF_BLK = 256
N_FF_BLK = D_FF // FF_BLK
DOWN_GROUP = 11
TILE_T = 256
SAMPLE_SEQS = 16
VMEM_LIMIT = 56 * 1024 * 1024


def _dot(a, b):
    return jnp.dot(a, b, preferred_element_type=F32)


def _dot_nt(a, b):
    return lax.dot_general(a, b, (((1,), (1,)), ((), ())), preferred_element_type=F32)


def _dot_tn(a, b):
    return lax.dot_general(a, b, (((0,), (0,)), ((), ())), preferred_element_type=F32)


def _split_bf16(x, parts):
    out = []
    r = x
    for i in range(parts):
        p = r.astype(BF16)
        out.append(p)
        if i + 1 < parts:
            r = r - p.astype(F32)
    return out


def _dot_exact_rhs(a_bf16, x, parts):
    acc = None
    for p in _split_bf16(x, parts):
        d = _dot(a_bf16, p)
        acc = d if acc is None else acc + d
    return acc


def _dot_exact_lhs(x, b_bf16, parts):
    acc = None
    for p in _split_bf16(x, parts):
        d = _dot(p, b_bf16)
        acc = d if acc is None else acc + d
    return acc


def _rms(x, g):
    ms = jnp.mean(x * x, axis=-1, keepdims=True)
    return x * lax.rsqrt(ms + EPS) * g


def _silu(x):
    return x * (1.0 / (1.0 + jnp.exp(-x)))


def _softplus(x):
    return jnp.maximum(x, 0.0) + jnp.log1p(jnp.exp(-jnp.abs(x)))


def _lane_lo(shape):
    return (lax.broadcasted_iota(jnp.int32, shape, len(shape) - 1) % 128) < 64


def _causal_conv(buf_ref, raw3, w_ref, b_ref, taps):
    length = raw3.shape[1]
    buf_ref[:, HIST:HIST + length, :] = raw3
    y = b_ref[...]
    for k in range(taps):
        lo = HIST - (taps - 1) + k
        y = y + buf_ref[:, lo:lo + length, :] * w_ref[k:k + 1, :]
    return y


def _causal_conv_rolled(raw, hist, w_ref, b_ref, taps):
    row = lax.broadcasted_iota(jnp.int32, hist.shape, 0)
    y = b_ref[...]
    for k in range(taps):
        s = taps - 1 - k
        if s == 0:
            term = raw
        else:
            rolled = pltpu.roll(raw, s, axis=0)
            head = jnp.where(row < s, pltpu.roll(hist, s, axis=0), rolled[0:HIST])
            term = jnp.concatenate([head, rolled[HIST:]], axis=0)
        y = y + term * w_ref[k:k + 1, :]
    return y


def _ssd_segment(acs, xs, dt, bm, cm, s2, dskip, smask, q):
    pad = CHUNK - q
    lane = lax.broadcasted_iota(jnp.int32, (q, D_SSD), 1)
    row = lax.broadcasted_iota(jnp.int32, (q, D_SSD), 0)
    rowf = jnp.sum(jnp.where((lane % CHUNK) == row, acs, 0.0), axis=0, keepdims=True)
    a_end = acs[q - 1:q, :]
    xdt = xs * dt
    xw = (xdt * jnp.exp(a_end - acs)).astype(BF16)
    cm_b = cm.astype(BF16)
    y_off = _dot(cm_b, s2.astype(BF16)) * jnp.exp(acs)
    zrow = jnp.zeros((2 * D_STATE - q, D_BC), F32)
    bm_t = jnp.concatenate([bm, zrow], axis=0).T.astype(BF16)
    xw_pad = jnp.concatenate([xw, jnp.zeros((2 * D_STATE - q, D_SSD), BF16)], axis=0)
    s_new = s2 * jnp.exp(a_end) + _dot(bm_t, xw_pad) * smask

    def stack2(a, b):
        if pad:
            zero = jnp.zeros((pad, a.shape[1]), a.dtype)
            return jnp.concatenate([a, zero, b, zero], axis=0)
        return jnp.concatenate([a, b], axis=0)

    lo128 = _lane_lo((q, 128))
    bstack = stack2(bm, bm).astype(BF16)
    lane128 = lax.broadcasted_iota(jnp.int32, (q, 128), 1)
    row128 = lax.broadcasted_iota(jnp.int32, (q, 128), 0)
    causal = (lane128 % CHUNK) <= row128
    y_pairs = []
    for g in range(2):
        cg = jnp.where(lo128 if g == 0 else ~lo128, cm, 0.0).astype(BF16)
        cb = _dot_nt(cg, bstack)
        for jj in range(2):
            j = 2 * g + jj
            col = acs[:, 128 * j:128 * (j + 1)]
            seg = col - rowf[:, 128 * j:128 * (j + 1)]
            decay = jnp.exp(jnp.where(causal, seg, -jnp.inf))
            m = (cb * decay).astype(BF16)
            xp = xdt[:, 128 * j:128 * (j + 1)]
            rhs = stack2(jnp.where(lo128, xp, 0.0), jnp.where(lo128, 0.0, xp)).astype(BF16)
            y_pairs.append(_dot(m, rhs))
    y = jnp.concatenate(y_pairs, axis=1) + y_off + dskip * xs
    return y, s_new


def _attn_segment(qn, kband, vband, bias_ref, sink_ref, first_valid, q):
    band = kband.shape[0]
    lo_b = _lane_lo((band, 128))
    lo_q = _lane_lo((q, 128))
    k_sw = pltpu.roll(kband, 64, axis=1)
    v_sw = pltpu.roll(vband, 64, axis=1)
    kdup = (jnp.where(lo_b, kband, k_sw).astype(BF16), jnp.where(lo_b, k_sw, kband).astype(BF16))
    vdup = (jnp.where(lo_b, vband, v_sw).astype(BF16), jnp.where(lo_b, v_sw, vband).astype(BF16))
    if first_valid is not None:
        valid = lax.broadcasted_iota(jnp.int32, (band, 1), 0) >= first_valid
    outs = []
    for g in range(2):
        p0 = qn[:, 256 * g:256 * g + 128]
        p1 = qn[:, 256 * g + 128:256 * g + 256]
        qs = jnp.concatenate([jnp.where(lo_q, p0, 0.0), jnp.where(lo_q, 0.0, p0),
                              jnp.where(lo_q, p1, 0.0), jnp.where(lo_q, 0.0, p1)], axis=0)
        s = _dot_nt(kdup[g], qs.astype(BF16)) * (HEAD_DIM ** -0.5) - bias_ref[g]
        if first_valid is not None:
            s = jnp.where(valid, s, -jnp.inf)
        sink = sink_ref[g]
        m = jnp.maximum(jnp.max(s, axis=0, keepdims=True), sink)
        p = jnp.exp(s - m)
        p = p / (jnp.sum(p, axis=0, keepdims=True) + jnp.exp(sink - m))
        og = _dot_tn(p.astype(BF16), vdup[g])
        outs.append(jnp.where(lo_q, og[0:q], og[q:2 * q]))
        outs.append(jnp.where(lo_q, og[2 * q:3 * q], og[3 * q:4 * q]))
    return jnp.concatenate(outs, axis=1)


def _state_rows(s2):
    s_t = s2.T
    half = D_SSD // 2
    return jnp.concatenate([s_t[0:half, 0:D_STATE], s_t[half:D_SSD, D_STATE:2 * D_STATE]], axis=0)


def _head_rms(x, ones_ref, g_ref):
    ssq = _dot_exact_lhs(x * x, ones_ref[...], 2)
    return x * lax.rsqrt(ssq * (1.0 / HEAD_DIM) + EPS) * g_ref[...]


def _in_proj(h, win_ref, lo, hi):
    return _dot(h, win_ref[:, lo:hi])


def _ada_kernel(c_ref, w_ref, b_ref, o_ref):
    c = c_ref[...]
    o_ref[0] = _dot(_silu(c).astype(BF16), w_ref[0].astype(BF16)) + b_ref[0]


def _ada_call(c_all, w_ada, b_ada):
    depth = w_ada.shape[0]
    n = c_all.shape[0]
    blk = 1024
    return pl.pallas_call(
        _ada_kernel,
        grid=(depth, 6 * D_MODEL // blk),
        in_specs=[
            pl.BlockSpec((n, D_MODEL), lambda l, j: (0, 0)),
            pl.BlockSpec((1, D_MODEL, blk), lambda l, j: (l, 0, j)),
            pl.BlockSpec((1, 1, blk), lambda l, j: (l, 0, j)),
        ],
        out_specs=pl.BlockSpec((1, n, blk), lambda l, j: (l, 0, j)),
        out_shape=jax.ShapeDtypeStruct((depth, n, 6 * D_MODEL), F32),
        compiler_params=pltpu.CompilerParams(
            dimension_semantics=("arbitrary", "arbitrary"), vmem_limit_bytes=VMEM_LIMIT),
        name="adaln",
    )(c_all, w_ada, b_ada.reshape(depth, 1, 6 * D_MODEL))


def _mixer_prompt_kernel(x_ref, mod_ref, nmix_ref, win_ref, cw_ref, cb_ref, dtb_ref, alog_ref,
                         dsk_ref, ssdn_ref, qn_ref, kn_ref, sink_ref, bias_ref, tril_ref,
                         onesq_ref, onesk_ref, smask_ref, wout_ref,
                         xo_ref, ssm_ref, ctail_ref, ko_ref, vo_ref,
                         xbc_buf, kbuf, vbuf, s2_ref, *, tile, n_tiles):
    t = pl.program_id(1)
    n_chunks = tile // CHUNK

    @pl.when(t == 0)
    def _():
        xbc_buf[:, 0:HIST, :] = jnp.zeros((1, HIST, D_XBC), F32)
        kbuf[0:WINDOW, :] = jnp.zeros((WINDOW, D_KV), F32)
        vbuf[0:WINDOW, :] = jnp.zeros((WINDOW, D_KV), F32)
        s2_ref[...] = jnp.zeros((2 * D_STATE, D_SSD), F32)

    x = x_ref[0]
    mod = mod_ref[0]
    sh1, sc1, g1 = mod[0:1], mod[1:2], mod[2:3]
    h = (_rms(x, nmix_ref[...]) * (1.0 + sc1) + sh1).astype(BF16)

    xbc_raw = _in_proj(h, win_ref, C_XBC, C_DT)
    conv = _causal_conv(xbc_buf, xbc_raw[None], cw_ref, cb_ref, SSD_CONV)[0]
    tail = xbc_buf[:, tile:tile + HIST, :]
    xbc_buf[:, 0:HIST, :] = tail
    xbc = _silu(conv)
    xs, bm, cm = xbc[:, 0:D_SSD], xbc[:, D_SSD:D_SSD + D_BC], xbc[:, D_SSD + D_BC:D_XBC]
    dt = _softplus(_in_proj(h, win_ref, C_DT, C_Q) + dtb_ref[...])
    acs = _dot_exact_rhs(tril_ref[...], dt * (-jnp.exp(alog_ref[...])), 3)
    smask = smask_ref[...]
    dskip = dsk_ref[...]
    ys = []
    for c in range(n_chunks):
        r = slice(c * CHUNK, (c + 1) * CHUNK)
        y_c, s_new = _ssd_segment(acs[r], xs[r], dt[r], bm[r], cm[r], s2_ref[...], dskip, smask,
                                  CHUNK)
        s2_ref[...] = s_new
        ys.append(y_c)
    z = _in_proj(h, win_ref, C_Z, C_XBC)
    y_ssd = _rms(jnp.concatenate(ys, axis=0) * _silu(z), ssdn_ref[...])

    qn = _head_rms(_in_proj(h, win_ref, C_Q, C_K), onesq_ref, qn_ref)
    kn = _head_rms(_in_proj(h, win_ref, C_K, C_V), onesk_ref, kn_ref)
    kbuf[WINDOW:WINDOW + tile, :] = kn
    vbuf[WINDOW:WINDOW + tile, :] = _in_proj(h, win_ref, C_V, C_END)
    band = WINDOW + CHUNK
    os_ = []
    for c in range(n_chunks):
        first_valid = (WINDOW // CHUNK - (t * n_chunks + c)) * CHUNK
        o_c = _attn_segment(qn[c * CHUNK:(c + 1) * CHUNK],
                            kbuf[c * CHUNK:c * CHUNK + band, :],
                            vbuf[c * CHUNK:c * CHUNK + band, :],
                            bias_ref, sink_ref, first_valid, CHUNK)
        os_.append(o_c)
    o = jnp.concatenate(os_, axis=0)
    k_last = kbuf[tile:tile + WINDOW, :]
    v_last = vbuf[tile:tile + WINDOW, :]
    kbuf[0:WINDOW, :] = k_last
    vbuf[0:WINDOW, :] = v_last

    mix = (_dot(y_ssd.astype(BF16), wout_ref[0:D_SSD, :])
           + _dot(o.astype(BF16), wout_ref[D_SSD:D_SSD + D_ATTN, :]))
    xo_ref[0] = x + g1 * mix

    @pl.when(t == n_tiles - 1)
    def _():
        ssm_ref[0] = _state_rows(s2_ref[...])
        ctail_ref[...] = tail
        ko_ref[0] = k_last
        vo_ref[0] = v_last


def _const_spec(shape):
    zeros = (0,) * len(shape)
    return pl.BlockSpec(shape, lambda *_: zeros, pipeline_mode=pl.Buffered(1))


def _mixer_prompt_call(x, mod, lw, consts, tile):
    b, length, _ = x.shape
    n_tiles = length // tile
    band = WINDOW + CHUNK
    kern = functools.partial(_mixer_prompt_kernel, tile=tile, n_tiles=n_tiles)
    per_b = lambda i, t: (i, 0, 0)
    return pl.pallas_call(
        kern,
        grid=(b, n_tiles),
        in_specs=[
            pl.BlockSpec((1, tile, D_MODEL), lambda i, t: (i, t, 0)),
            pl.BlockSpec((1, 8, D_MODEL), per_b),
            _const_spec((1, D_MODEL)),
            _const_spec((D_MODEL, C_END)),
            _const_spec((SSD_CONV, D_XBC)),
            _const_spec((1, D_XBC)),
            _const_spec((1, D_SSD)),
            _const_spec((1, D_SSD)),
            _const_spec((1, D_SSD)),
            _const_spec((1, D_SSD)),
            _const_spec((1, D_ATTN)),
            _const_spec((1, D_KV)),
            _const_spec((2, 1, 4 * CHUNK)),
            _const_spec((2, band, 4 * CHUNK)),
            _const_spec((tile, tile)),
            _const_spec((D_ATTN, D_ATTN)),
            _const_spec((D_KV, D_KV)),
            _const_spec((2 * D_STATE, D_SSD)),
            _const_spec((D_SSD + D_ATTN, D_MODEL)),
        ],
        out_specs=[
            pl.BlockSpec((1, tile, D_MODEL), lambda i, t: (i, t, 0)),
            pl.BlockSpec((1, D_SSD, D_STATE), per_b),
            pl.BlockSpec((1, HIST, D_XBC), per_b),
            pl.BlockSpec((1, WINDOW, D_KV), per_b),
            pl.BlockSpec((1, WINDOW, D_KV), per_b),
        ],
        out_shape=[
            jax.ShapeDtypeStruct((b, length, D_MODEL), F32),
            jax.ShapeDtypeStruct((b, D_SSD, D_STATE), F32),
            jax.ShapeDtypeStruct((b, HIST, D_XBC), F32),
            jax.ShapeDtypeStruct((b, WINDOW, D_KV), F32),
            jax.ShapeDtypeStruct((b, WINDOW, D_KV), F32),
        ],
        scratch_shapes=[
            pltpu.VMEM((1, HIST + tile, D_XBC), F32),
            pltpu.VMEM((WINDOW + tile, D_KV), F32),
            pltpu.VMEM((WINDOW + tile, D_KV), F32),
            pltpu.VMEM((2 * D_STATE, D_SSD), F32),
        ],
        compiler_params=pltpu.CompilerParams(
            dimension_semantics=("arbitrary", "arbitrary"), vmem_limit_bytes=VMEM_LIMIT),
        name="mixer_prompt",
    )(x, mod, lw["norm_mix"], lw["w_in"], lw["conv_w"], lw["conv_b"], lw["dt_bias"], lw["a_log"],
      lw["d_skip"], lw["ssd_norm"], lw["q_norm"], lw["k_norm"], consts["sink_p"], consts["bias_p"],
      consts["tril_p"], consts["ones_q"], consts["ones_k"], consts["smask"], lw["w_out"])


def _mixer_sample_kernel(x_ref, sh_ref, sc_ref, g_ref, ssm_in_ref, ccache_ref, kc_ref, vc_ref,
                         nmix_ref, win_ref, cw_ref, cb_ref, dtb_ref, alog_ref,
                         dsk_ref, ssdn_ref, qn_ref, kn_ref, sink_ref, bias_ref, tril_ref,
                         onesq_ref, onesk_ref, smask_ref, wout_ref,
                         xo_ref, ssm_ref, ctail_ref, ko_ref, vo_ref,
                         xbc_buf, xs_buf, bc_buf, dt_buf, acs_buf, q_buf, k_buf, v_buf, y_buf, o_buf,
                         *, n_seq, length):
    rows = n_seq * length
    x = x_ref[...]
    h = (_rms(x, nmix_ref[...]) * (1.0 + sc_ref[...]) + sh_ref[...]).astype(BF16)

    xbc_raw = _in_proj(h, win_ref, C_XBC, C_DT)
    xbc_buf[:, 0:HIST, :] = ccache_ref[...]
    conv = _causal_conv(xbc_buf, xbc_raw.reshape(n_seq, length, D_XBC), cw_ref, cb_ref, SSD_CONV)
    ctail_ref[...] = xbc_buf[:, length:length + HIST, :]
    xbc = _silu(conv.reshape(rows, D_XBC))
    xs_buf[...] = xbc[:, 0:D_SSD]
    bc_buf[...] = xbc[:, D_SSD:D_XBC]
    dt = _softplus(_in_proj(h, win_ref, C_DT, C_Q) + dtb_ref[...])
    dt_buf[...] = dt
    acs_buf[...] = _dot_exact_rhs(tril_ref[...], dt * (-jnp.exp(alog_ref[...])), 3)
    q_buf[...] = _head_rms(_in_proj(h, win_ref, C_Q, C_K), onesq_ref, qn_ref)
    k_buf[...] = _head_rms(_in_proj(h, win_ref, C_K, C_V), onesk_ref, kn_ref)
    v_buf[...] = _in_proj(h, win_ref, C_V, C_END)

    def body(i, carry):
        r = pl.ds(pl.multiple_of(i * length, length), length)
        bc = bc_buf[r, :]
        y_i, s_new = _ssd_segment(acs_buf[r, :], xs_buf[r, :], dt_buf[r, :], bc[:, 0:D_BC],
                                  bc[:, D_BC:2 * D_BC], ssm_in_ref[i], dsk_ref[...],
                                  smask_ref[...], length)
        ssm_ref[i] = _state_rows(s_new)
        y_buf[r, :] = y_i
        kband = jnp.concatenate([kc_ref[i], k_buf[r, :]], axis=0)
        vband = jnp.concatenate([vc_ref[i], v_buf[r, :]], axis=0)
        ko_ref[i] = kband[length:length + WINDOW]
        vo_ref[i] = vband[length:length + WINDOW]
        o_buf[r, :] = _attn_segment(q_buf[r, :], kband, vband, bias_ref, sink_ref, None, length)
        return carry

    lax.fori_loop(0, n_seq, body, 0)

    z = _in_proj(h, win_ref, C_Z, C_XBC)
    y_ssd = _rms(y_buf[...] * _silu(z), ssdn_ref[...])
    mix = (_dot(y_ssd.astype(BF16), wout_ref[0:D_SSD, :])
           + _dot(o_buf[...].astype(BF16), wout_ref[D_SSD:D_SSD + D_ATTN, :]))
    xo_ref[...] = x + g_ref[...] * mix


def _lead_spec(shape):
    tail_zeros = (0,) * (len(shape) - 1)
    return pl.BlockSpec(shape, lambda i: (i,) + tail_zeros)


def _mixer_sample_call(x2, sh, sc, g, ssm_in, ccache, kc, vc, lw, consts, n_blk, length):
    n_seq = ssm_in.shape[0]
    rows = n_blk * length
    kern = functools.partial(_mixer_sample_kernel, n_seq=n_blk, length=length)
    row_args = (x2, sh, sc, g)
    seq_args = (ssm_in, ccache, kc, vc)
    const_args = (lw["norm_mix"], lw["w_in"], lw["conv_w"], lw["conv_b"], lw["dt_bias"],
                  lw["a_log"], lw["d_skip"], lw["ssd_norm"], lw["q_norm"], lw["k_norm"],
                  consts["sink_s"], consts["bias_s"], consts["tril_s"], consts["ones_q"],
                  consts["ones_k"], consts["smask"], lw["w_out"])
    out_shape = [
        jax.ShapeDtypeStruct((n_seq * length, D_MODEL), F32),
        jax.ShapeDtypeStruct((n_seq, D_SSD, D_STATE), F32),
        jax.ShapeDtypeStruct((n_seq, HIST, D_XBC), F32),
        jax.ShapeDtypeStruct((n_seq, WINDOW, D_KV), F32),
        jax.ShapeDtypeStruct((n_seq, WINDOW, D_KV), F32),
    ]
    return pl.pallas_call(
        kern,
        grid=(n_seq // n_blk,),
        in_specs=([_lead_spec((rows, D_MODEL)) for _ in row_args]
                  + [_lead_spec((n_blk,) + a.shape[1:]) for a in seq_args]
                  + [_const_spec(a.shape) for a in const_args]),
        out_specs=([_lead_spec((rows, D_MODEL))]
                   + [_lead_spec((n_blk,) + s.shape[1:]) for s in out_shape[1:]]),
        out_shape=out_shape,
        scratch_shapes=[
            pltpu.VMEM((n_blk, HIST + length, D_XBC), F32),
            pltpu.VMEM((rows, D_SSD), F32),
            pltpu.VMEM((rows, 2 * D_BC), F32),
            pltpu.VMEM((rows, D_SSD), F32),
            pltpu.VMEM((rows, D_SSD), F32),
            pltpu.VMEM((rows, D_ATTN), F32),
            pltpu.VMEM((rows, D_KV), F32),
            pltpu.VMEM((rows, D_KV), F32),
            pltpu.VMEM((rows, D_SSD), F32),
            pltpu.VMEM((rows, D_ATTN), F32),
        ],
        compiler_params=pltpu.CompilerParams(
            dimension_semantics=("arbitrary",), vmem_limit_bytes=VMEM_LIMIT),
        name="mixer_sample",
    )(*row_args, *seq_args, *const_args)


def _ffn_body(x, sh, sc, g, hist_ref, nffn_ref, wup_ref, cw_ref, cb_ref, wdn_ref, tail_ref,
              ubuf, gbuf, n_seq, length):
    rows = n_seq * length
    h = (_rms(x, nffn_ref[...]) * (1.0 + sc) + sh).astype(BF16)
    def up(j):
        return [_dot(h, wup_ref[:, base + j * FF_BLK:base + (j + 1) * FF_BLK]) for base in (0, D_FF)]

    acc = None
    acts = []
    raws = up(0)
    for j in range(N_FF_BLK):
        nxt = up(j + 1) if j + 1 < N_FF_BLK else None
        halves = []
        for raw, buf, base in zip(raws, (ubuf, gbuf), (0, D_FF)):
            lo = base + j * FF_BLK
            cw, cb = cw_ref.at[:, lo:lo + FF_BLK], cb_ref.at[:, lo:lo + FF_BLK]
            if n_seq == 1:
                y = _causal_conv_rolled(raw, hist_ref[0, :, lo:lo + FF_BLK], cw, cb, FFN_CONV)
                tail_ref[0, :, lo:lo + FF_BLK] = raw[length - HIST:length]
            else:
                buf[:, 0:HIST, :] = hist_ref[:, :, lo:lo + FF_BLK]
                y = _causal_conv(buf, raw.reshape(n_seq, length, FF_BLK), cw, cb, FFN_CONV)
                tail_ref[:, :, lo:lo + FF_BLK] = buf[:, length:length + HIST, :]
            halves.append(y.reshape(rows, FF_BLK))
        acts.append((_silu(halves[1]) * halves[0]).astype(BF16))
        raws = nxt
        if len(acts) == DOWN_GROUP or j + 1 == N_FF_BLK:
            k0 = (j + 1 - len(acts)) * FF_BLK
            d = _dot(jnp.concatenate(acts, axis=1), wdn_ref[k0:(j + 1) * FF_BLK, :])
            acc = d if acc is None else acc + d
            acts = []
    return x + g * acc


def _ffn_prompt_kernel(x_ref, mod_ref, nffn_ref, wup_ref, cw_ref, cb_ref, wdn_ref,
                       xo_ref, tail_ref, hist, *, tile, n_tiles):
    t = pl.program_id(1)

    @pl.when(t == 0)
    def _():
        hist[...] = jnp.zeros(hist.shape, F32)

    mod = mod_ref[0]
    xo_ref[0] = _ffn_body(x_ref[0], mod[3:4], mod[4:5], mod[5:6], hist, nffn_ref, wup_ref, cw_ref,
                          cb_ref, wdn_ref, hist, None, None, 1, tile)

    @pl.when(t == n_tiles - 1)
    def _():
        tail_ref[...] = hist[...]


def _ffn_prompt_call(x, mod, lw, tile):
    b, length, _ = x.shape
    n_tiles = length // tile
    kern = functools.partial(_ffn_prompt_kernel, tile=tile, n_tiles=n_tiles)
    per_b = lambda i, t: (i, 0, 0)
    return pl.pallas_call(
        kern,
        grid=(b, n_tiles),
        in_specs=[
            pl.BlockSpec((1, tile, D_MODEL), lambda i, t: (i, t, 0)),
            pl.BlockSpec((1, 8, D_MODEL), per_b),
            _const_spec((1, D_MODEL)),
            _const_spec((D_MODEL, 2 * D_FF)),
            _const_spec((FFN_CONV, 2 * D_FF)),
            _const_spec((1, 2 * D_FF)),
            _const_spec((D_FF, D_MODEL)),
        ],
        out_specs=[
            pl.BlockSpec((1, tile, D_MODEL), lambda i, t: (i, t, 0)),
            pl.BlockSpec((1, HIST, 2 * D_FF), per_b),
        ],
        out_shape=[
            jax.ShapeDtypeStruct((b, length, D_MODEL), F32),
            jax.ShapeDtypeStruct((b, HIST, 2 * D_FF), F32),
        ],
        scratch_shapes=[
            pltpu.VMEM((1, HIST, 2 * D_FF), F32),
        ],
        compiler_params=pltpu.CompilerParams(
            dimension_semantics=("arbitrary", "arbitrary"), vmem_limit_bytes=VMEM_LIMIT),
        name="ffn_prompt",
    )(x, mod, lw["norm_ffn"], lw["w_up"], lw["conv_ffn_w"], lw["conv_ffn_b"], lw["w_down"])


def _ffn_sample_kernel(x_ref, sh_ref, sc_ref, g_ref, hist_ref, nffn_ref, wup_ref, cw_ref, cb_ref,
                       wdn_ref, xo_ref, tail_ref, ubuf, gbuf, *, n_seq, length):
    xo_ref[...] = _ffn_body(x_ref[...], sh_ref[...], sc_ref[...], g_ref[...], hist_ref, nffn_ref,
                            wup_ref, cw_ref, cb_ref, wdn_ref, tail_ref, ubuf, gbuf, n_seq, length)


def _ffn_sample_call(x2, sh, sc, g, hist, lw, n_blk, length):
    n_seq = hist.shape[0]
    rows = n_blk * length
    kern = functools.partial(_ffn_sample_kernel, n_seq=n_blk, length=length)
    row_args = (x2, sh, sc, g)
    const_args = (lw["norm_ffn"], lw["w_up"], lw["conv_ffn_w"], lw["conv_ffn_b"], lw["w_down"])
    out_shape = [
        jax.ShapeDtypeStruct((n_seq * length, D_MODEL), F32),
        jax.ShapeDtypeStruct((n_seq, HIST, 2 * D_FF), F32),
    ]
    return pl.pallas_call(
        kern,
        grid=(n_seq // n_blk,),
        in_specs=([_lead_spec((rows, D_MODEL)) for _ in row_args]
                  + [_lead_spec((n_blk, HIST, 2 * D_FF))]
                  + [_const_spec(a.shape) for a in const_args]),
        out_specs=[_lead_spec((rows, D_MODEL)), _lead_spec((n_blk, HIST, 2 * D_FF))],
        out_shape=out_shape,
        scratch_shapes=[
            pltpu.VMEM((n_blk, HIST + length, FF_BLK), F32),
            pltpu.VMEM((n_blk, HIST + length, FF_BLK), F32),
        ],
        compiler_params=pltpu.CompilerParams(
            dimension_semantics=("arbitrary",), vmem_limit_bytes=VMEM_LIMIT),
        name="ffn_sample",
    )(*row_args, hist, *const_args)


def _expand_heads(v):
    return jnp.repeat(v.astype(F32), SSD_HEAD_DIM)[None, :]


def _prep_layer(l, w_in, conv_ssd_w, conv_ssd_b, dt_bias, a_log, d_skip, ssd_norm, q_norm, k_norm,
                w_out, norm_mix, norm_ffn, w_up, conv_ffn_w, conv_ffn_b, w_down):
    wi = w_in[l]
    o_xbc, o_dt = D_SSD, D_SSD + D_XBC
    o_q = o_dt + SSD_HEADS
    o_k, o_v = o_q + D_ATTN, o_q + D_ATTN + D_KV
    w_dt = jnp.repeat(wi[:, o_dt:o_q], SSD_HEAD_DIM, axis=1)
    w_in_r = jnp.concatenate(
        [wi[:, :o_xbc], wi[:, o_xbc:o_dt], w_dt, wi[:, o_q:o_k], wi[:, o_k:o_v], wi[:, o_v:]],
        axis=1).astype(BF16)
    return {
        "norm_mix": norm_mix[l][None, :],
        "w_in": w_in_r,
        "conv_w": conv_ssd_w[l],
        "conv_b": conv_ssd_b[l][None, :],
        "dt_bias": _expand_heads(dt_bias[l]),
        "a_log": _expand_heads(a_log[l]),
        "d_skip": _expand_heads(d_skip[l]),
        "ssd_norm": ssd_norm[l][None, :],
        "q_norm": jnp.tile(q_norm[l], ATTN_HEADS)[None, :],
        "k_norm": jnp.tile(k_norm[l], D_KV // HEAD_DIM)[None, :],
        "w_out": w_out[l].astype(BF16),
        "norm_ffn": norm_ffn[l][None, :],
        "w_up": w_up[l].astype(BF16),
        "conv_ffn_w": conv_ffn_w[l],
        "conv_ffn_b": conv_ffn_b[l][None, :],
        "w_down": w_down[l].astype(BF16),
    }


def _block_tril(n, blk):
    i = np.arange(n)
    return jnp.asarray(((i[:, None] // blk) == (i[None, :] // blk)) & (i[None, :] <= i[:, None]),
                       dtype=BF16)


def _block_ones(n, blk):
    i = np.arange(n)
    return jnp.asarray((i[:, None] // blk) == (i[None, :] // blk), dtype=BF16)


def _alibi_bias(q):
    slopes = (2.0 ** (-8.0 * np.arange(1, ATTN_HEADS + 1) / ATTN_HEADS)).astype(np.float32)
    dist = np.abs(np.arange(q)[:, None] + WINDOW - np.arange(WINDOW + q)[None, :]).astype(np.float32)
    bias = slopes[:, None, None] * dist[None]
    return jnp.asarray(np.transpose(bias.reshape(2, 4 * q, WINDOW + q), (0, 2, 1)))


def _sink_rows(sinks_l, q):
    return jnp.repeat(sinks_l.astype(F32), q).reshape(2, 1, 4 * q)


def _state_mask():
    r = np.arange(2 * D_STATE)[:, None] // D_STATE
    c = np.arange(D_SSD)[None, :] // (D_SSD // 2)
    return jnp.asarray(r == c, dtype=F32)


def _state_to_s2(state):
    n = state.shape[0]
    st = jnp.transpose(state, (0, 3, 1, 2)).reshape(n, D_STATE, D_SSD)
    half = D_SSD // 2
    zero = jnp.zeros((n, D_STATE, half), state.dtype)
    top = jnp.concatenate([st[:, :, :half], zero], axis=2)
    bot = jnp.concatenate([zero, st[:, :, half:]], axis=2)
    return jnp.concatenate([top, bot], axis=1)


def _pad_hist(cache):
    n, k, c = cache.shape
    return jnp.concatenate([jnp.zeros((n, HIST - k, c), cache.dtype), cache], axis=1)


def kernel(x_prompt, x_sample, c_prompt, c_sample, state_ssm, cache_conv_ssd, cache_attn_k,
           cache_attn_v, cache_conv_ffn, w_ada, b_ada, norm_mix, w_in, conv_ssd_w, conv_ssd_b,
           dt_bias, a_log, d_skip, ssd_norm, q_norm, k_norm, sinks, w_out, norm_ffn, w_up,
           conv_ffn_w, conv_ffn_b, w_down):
    depth = w_ada.shape[0]
    bp, seq, _ = x_prompt.shape
    bs, lsamp, _ = x_sample.shape
    rows_s = bs * lsamp
    tile = TILE_T
    n_blk = SAMPLE_SEQS

    mod_all = _ada_call(jnp.concatenate([c_prompt, c_sample], axis=0), w_ada, b_ada)
    mod_all = mod_all.reshape(depth, bp + bs, 6, D_MODEL)
    mod_p = jnp.pad(mod_all[:, :bp], ((0, 0), (0, 0), (0, 2), (0, 0)))
    mod_s = jnp.repeat(mod_all[:, bp:], lsamp, axis=1)

    consts = {
        "tril_p": _block_tril(tile, CHUNK),
        "tril_s": _block_tril(n_blk * lsamp, lsamp),
        "ones_q": _block_ones(D_ATTN, HEAD_DIM),
        "ones_k": _block_ones(D_KV, HEAD_DIM),
        "smask": _state_mask(),
        "bias_p": _alibi_bias(CHUNK),
        "bias_s": _alibi_bias(lsamp),
    }

    xp = x_prompt
    xs = x_sample.reshape(rows_s, D_MODEL)
    outs = [[] for _ in range(10)]
    for l in range(depth):
        lw = _prep_layer(l, w_in, conv_ssd_w, conv_ssd_b, dt_bias, a_log, d_skip, ssd_norm, q_norm,
                         k_norm, w_out, norm_mix, norm_ffn, w_up, conv_ffn_w, conv_ffn_b, w_down)
        consts["sink_p"] = _sink_rows(sinks[l], CHUNK)
        consts["sink_s"] = _sink_rows(sinks[l], lsamp)

        xp, ssm_p, ct_p, k_p, v_p = _mixer_prompt_call(xp, mod_p[l], lw, consts, tile)
        xp, ft_p = _ffn_prompt_call(xp, mod_p[l], lw, tile)

        ms = mod_s[l]
        xs, ssm_s, ct_s, k_s, v_s = _mixer_sample_call(
            xs, ms[:, 0], ms[:, 1], ms[:, 2], _state_to_s2(state_ssm[l]),
            _pad_hist(cache_conv_ssd[l]), cache_attn_k[l].reshape(bs, WINDOW, D_KV),
            cache_attn_v[l].reshape(bs, WINDOW, D_KV), lw, consts, n_blk, lsamp)
        xs, ft_s = _ffn_sample_call(xs, ms[:, 3], ms[:, 4], ms[:, 5], _pad_hist(cache_conv_ffn[l]),
                                    lw, n_blk, lsamp)

        kv_shape = (WINDOW, D_KV // HEAD_DIM, HEAD_DIM)
        for lst, val in zip(outs, (
                ssm_p.reshape(bp, SSD_HEADS, SSD_HEAD_DIM, D_STATE),
                ssm_s.reshape(bs, SSD_HEADS, SSD_HEAD_DIM, D_STATE),
                ct_p[:, HIST - (SSD_CONV - 1):], ct_s[:, HIST - (SSD_CONV - 1):],
                k_p.reshape((bp,) + kv_shape), k_s.reshape((bs,) + kv_shape),
                v_p.reshape((bp,) + kv_shape), v_s.reshape((bs,) + kv_shape),
                ft_p[:, HIST - (FFN_CONV - 1):], ft_s[:, HIST - (FFN_CONV - 1):])):
            lst.append(val)

    return (xp, xs.reshape(bs, lsamp, D_MODEL)) + tuple(jnp.stack(o) for o in outs)
```

```python
import functools

import numpy as np
import jax
import jax.numpy as jnp
from jax import lax
from jax.experimental import pallas as pl
from jax.experimental.pallas import tpu as pltpu

F32 = jnp.float32
BF16 = jnp.bfloat16

D_MODEL = 1024
SSD_HEADS = 8
SSD_HEAD_DIM = 64
D_SSD = 512
D_STATE = 64
D_BC = 128
SSD_CONV = 4
D_XBC = 768
ATTN_HEADS = 8
HEAD_DIM = 64
D_ATTN = 512
D_KV = 128
WINDOW = 128
CHUNK = 64
D_FF = 2816
FFN_CONV = 3
EPS = 1e-6

C_Z, C_XBC, C_DT, C_Q, C_K, C_V, C_END = 0, 512, 1280, 1792, 2304, 2432, 2560

HIST = 8
FF_BLK = 256
N_FF_BLK = D_FF // FF_BLK
TILE_T = 256
SAMPLE_SEQS = 16
VMEM_LIMIT = 56 * 1024 * 1024


def _dot(a, b):
    return jnp.dot(a, b, preferred_element_type=F32)


def _dot_nt(a, b):
    return lax.dot_general(a, b, (((1,), (1,)), ((), ())), preferred_element_type=F32)


def _dot_tn(a, b):
    return lax.dot_general(a, b, (((0,), (0,)), ((), ())), preferred_element_type=F32)


def _split_bf16(x, parts):
    out = []
    r = x
    for i in range(parts):
        p = r.astype(BF16)
        out.append(p)
        if i + 1 < parts:
            r = r - p.astype(F32)
    return out


def _dot_exact_rhs(a_bf16, x, parts):
    acc = None
    for p in _split_bf16(x, parts):
        d = _dot(a_bf16, p)
        acc = d if acc is None else acc + d
    return acc


def _dot_exact_lhs(x, b_bf16, parts):
    acc = None
    for p in _split_bf16(x, parts):
        d = _dot(p, b_bf16)
        acc = d if acc is None else acc + d
    return acc


def _rms(x, g):
    ms = jnp.mean(x * x, axis=-1, keepdims=True)
    return x * lax.rsqrt(ms + EPS) * g


def _silu(x):
    return x * (1.0 / (1.0 + jnp.exp(-x)))


def _softplus(x):
    return jnp.maximum(x, 0.0) + jnp.log1p(jnp.exp(-jnp.abs(x)))


def _lane_lo(shape):
    return (lax.broadcasted_iota(jnp.int32, shape, len(shape) - 1) % 128) < 64


def _causal_conv(buf_ref, raw3, w_ref, b_ref, taps):
    length = raw3.shape[1]
    buf_ref[:, HIST:HIST + length, :] = raw3
    y = b_ref[...]
    for k in range(taps):
        lo = HIST - (taps - 1) + k
        y = y + buf_ref[:, lo:lo + length, :] * w_ref[k:k + 1, :]
    return y


def _causal_conv_rolled(raw, hist, w_ref, b_ref, taps):
    row = lax.broadcasted_iota(jnp.int32, hist.shape, 0)
    y = b_ref[...]
    for k in range(taps):
        s = taps - 1 - k
        if s == 0:
            term = raw
        else:
            rolled = pltpu.roll(raw, s, axis=0)
            head = jnp.where(row < s, pltpu.roll(hist, s, axis=0), rolled[0:HIST])
            term = jnp.concatenate([head, rolled[HIST:]], axis=0)
        y = y + term * w_ref[k:k + 1, :]
    return y


def _ssd_segment(acs, xs, dt, bm, cm, s2, dskip, smask, q):
    pad = CHUNK - q
    lane = lax.broadcasted_iota(jnp.int32, (q, D_SSD), 1)
    row = lax.broadcasted_iota(jnp.int32, (q, D_SSD), 0)
    rowf = jnp.sum(jnp.where((lane % CHUNK) == row, acs, 0.0), axis=0, keepdims=True)
    a_end = acs[q - 1:q, :]
    xdt = xs * dt
    xw = (xdt * jnp.exp(a_end - acs)).astype(BF16)
    cm_b = cm.astype(BF16)
    y_off = _dot(cm_b, s2.astype(BF16)) * jnp.exp(acs)
    zrow = jnp.zeros((2 * D_STATE - q, D_BC), F32)
    bm_t = jnp.concatenate([bm, zrow], axis=0).T.astype(BF16)
    xw_pad = jnp.concatenate([xw, jnp.zeros((2 * D_STATE - q, D_SSD), BF16)], axis=0)
    s_new = s2 * jnp.exp(a_end) + _dot(bm_t, xw_pad) * smask

    def stack2(a, b):
        if pad:
            zero = jnp.zeros((pad, a.shape[1]), a.dtype)
            return jnp.concatenate([a, zero, b, zero], axis=0)
        return jnp.concatenate([a, b], axis=0)

    lo128 = _lane_lo((q, 128))
    bstack = stack2(bm, bm).astype(BF16)
    lane128 = lax.broadcasted_iota(jnp.int32, (q, 128), 1)
    row128 = lax.broadcasted_iota(jnp.int32, (q, 128), 0)
    causal = (lane128 % CHUNK) <= row128
    y_pairs = []
    for g in range(2):
        cg = jnp.where(lo128 if g == 0 else ~lo128, cm, 0.0).astype(BF16)
        cb = _dot_nt(cg, bstack)
        for jj in range(2):
            j = 2 * g + jj
            col = acs[:, 128 * j:128 * (j + 1)]
            seg = col - rowf[:, 128 * j:128 * (j + 1)]
            decay = jnp.exp(jnp.where(causal, seg, -jnp.inf))
            m = (cb * decay).astype(BF16)
            xp = xdt[:, 128 * j:128 * (j + 1)]
            rhs = stack2(jnp.where(lo128, xp, 0.0), jnp.where(lo128, 0.0, xp)).astype(BF16)
            y_pairs.append(_dot(m, rhs))
    y = jnp.concatenate(y_pairs, axis=1) + y_off + dskip * xs
    return y, s_new


def _attn_segment(qn, kband, vband, bias_ref, sink_ref, first_valid, q):
    band = kband.shape[0]
    lo_b = _lane_lo((band, 128))
    lo_q = _lane_lo((q, 128))
    k_sw = pltpu.roll(kband, 64, axis=1)
    v_sw = pltpu.roll(vband, 64, axis=1)
    kdup = (jnp.where(lo_b, kband, k_sw).astype(BF16), jnp.where(lo_b, k_sw, kband).astype(BF16))
    vdup = (jnp.where(lo_b, vband, v_sw).astype(BF16), jnp.where(lo_b, v_sw, vband).astype(BF16))
    if first_valid is not None:
        valid = lax.broadcasted_iota(jnp.int32, (band, 1), 0) >= first_valid
    outs = []
    for g in range(2):
        p0 = qn[:, 256 * g:256 * g + 128]
        p1 = qn[:, 256 * g + 128:256 * g + 256]
        qs = jnp.concatenate([jnp.where(lo_q, p0, 0.0), jnp.where(lo_q, 0.0, p0),
                              jnp.where(lo_q, p1, 0.0), jnp.where(lo_q, 0.0, p1)], axis=0)
        s = _dot_nt(kdup[g], qs.astype(BF16)) * (HEAD_DIM ** -0.5) - bias_ref[g]
        if first_valid is not None:
            s = jnp.where(valid, s, -jnp.inf)
        sink = sink_ref[g]
        m = jnp.maximum(jnp.max(s, axis=0, keepdims=True), sink)
        p = jnp.exp(s - m)
        p = p / (jnp.sum(p, axis=0, keepdims=True) + jnp.exp(sink - m))
        og = _dot_tn(p.astype(BF16), vdup[g])
        outs.append(jnp.where(lo_q, og[0:q], og[q:2 * q]))
        outs.append(jnp.where(lo_q, og[2 * q:3 * q], og[3 * q:4 * q]))
    return jnp.concatenate(outs, axis=1)


def _state_rows(s2):
    s_t = s2.T
    half = D_SSD // 2
    return jnp.concatenate([s_t[0:half, 0:D_STATE], s_t[half:D_SSD, D_STATE:2 * D_STATE]], axis=0)


def _head_rms(x, ones_ref, g_ref):
    ssq = _dot_exact_lhs(x * x, ones_ref[...], 2)
    return x * lax.rsqrt(ssq * (1.0 / HEAD_DIM) + EPS) * g_ref[...]


def _in_proj(h, win_ref, lo, hi):
    return _dot(h, win_ref[:, lo:hi])


def _run_interleaved(*gens):
    live = list(gens)
    while live:
        for g in list(live):
            try:
                next(g)
            except StopIteration:
                live.remove(g)


def _ada_kernel(c_ref, w_ref, b_ref, o_ref):
    c = c_ref[...]
    o_ref[0] = _dot(_silu(c).astype(BF16), w_ref[0].astype(BF16)) + b_ref[0]


def _ada_call(c_all, w_ada, b_ada):
    depth = w_ada.shape[0]
    n = c_all.shape[0]
    blk = 1024
    return pl.pallas_call(
        _ada_kernel,
        grid=(depth, 6 * D_MODEL // blk),
        in_specs=[
            pl.BlockSpec((n, D_MODEL), lambda l, j: (0, 0)),
            pl.BlockSpec((1, D_MODEL, blk), lambda l, j: (l, 0, j)),
            pl.BlockSpec((1, 1, blk), lambda l, j: (l, 0, j)),
        ],
        out_specs=pl.BlockSpec((1, n, blk), lambda l, j: (l, 0, j)),
        out_shape=jax.ShapeDtypeStruct((depth, n, 6 * D_MODEL), F32),
        compiler_params=pltpu.CompilerParams(
            dimension_semantics=("arbitrary", "arbitrary"), vmem_limit_bytes=VMEM_LIMIT),
        name="adaln",
    )(c_all, w_ada, b_ada.reshape(depth, 1, 6 * D_MODEL))


def _ffn_steps(res, x, sh, sc, g, hist_ref, nffn_ref, wup_ref, cw_ref, cb_ref, wdn_ref, tail_ref,
               ubuf, gbuf, n_seq, length):
    rows = n_seq * length
    h = (_rms(x, nffn_ref[...]) * (1.0 + sc) + sh).astype(BF16)

    def up(j):
        return [_dot(h, wup_ref[:, base + j * FF_BLK:base + (j + 1) * FF_BLK]) for base in (0, D_FF)]

    acts = []
    raws = up(0)
    yield
    for j in range(N_FF_BLK):
        nxt = up(j + 1) if j + 1 < N_FF_BLK else None
        halves = []
        for raw, buf, base in zip(raws, (ubuf, gbuf), (0, D_FF)):
            lo = base + j * FF_BLK
            cw, cb = cw_ref.at[:, lo:lo + FF_BLK], cb_ref.at[:, lo:lo + FF_BLK]
            if n_seq == 1:
                y = _causal_conv_rolled(raw, hist_ref[0, :, lo:lo + FF_BLK], cw, cb, FFN_CONV)
                tail_ref[0, :, lo:lo + FF_BLK] = raw[length - HIST:length]
            else:
                buf[:, 0:HIST, :] = hist_ref[:, :, lo:lo + FF_BLK]
                y = _causal_conv(buf, raw.reshape(n_seq, length, FF_BLK), cw, cb, FFN_CONV)
                tail_ref[:, :, lo:lo + FF_BLK] = buf[:, length:length + HIST, :]
            halves.append(y.reshape(rows, FF_BLK))
        acts.append((_silu(halves[1]) * halves[0]).astype(BF16))
        raws = nxt
        yield
    act = jnp.concatenate(acts, axis=1)
    outs = []
    for n in range(D_MODEL // FF_BLK):
        cols = slice(n * FF_BLK, (n + 1) * FF_BLK)
        outs.append(x[:, cols] + g[:, cols] * _dot(act, wdn_ref[:, cols]))
        yield
    res["out"] = jnp.concatenate(outs, axis=1)


def _mixer_prompt_steps(res, x, mod, t, nmix_ref, win_ref, cw_ref, cb_ref, dtb_ref, alog_ref,
                        dsk_ref, ssdn_ref, qn_ref, kn_ref, sink_ref, bias_ref, tril_ref, onesq_ref,
                        onesk_ref, smask_ref, wout_ref, xbc_buf, kbuf, vbuf, s2_ref, tile):
    n_chunks = tile // CHUNK
    sh1, sc1, g1 = mod[0:1], mod[1:2], mod[2:3]
    h = (_rms(x, nmix_ref[...]) * (1.0 + sc1) + sh1).astype(BF16)

    xbc_raw = _in_proj(h, win_ref, C_XBC, C_DT)
    yield
    conv = _causal_conv_rolled(xbc_raw, xbc_buf[...], cw_ref, cb_ref, SSD_CONV)
    res["tail"] = xbc_raw[tile - HIST:tile]
    xbc_buf[...] = res["tail"]
    xbc = _silu(conv)
    xs, bm, cm = xbc[:, 0:D_SSD], xbc[:, D_SSD:D_SSD + D_BC], xbc[:, D_SSD + D_BC:D_XBC]
    yield
    dt = _softplus(_in_proj(h, win_ref, C_DT, C_Q) + dtb_ref[...])
    acs = _dot_exact_rhs(tril_ref[...], dt * (-jnp.exp(alog_ref[...])), 3)
    yield
    qn = _head_rms(_in_proj(h, win_ref, C_Q, C_K), onesq_ref, qn_ref)
    kn = _head_rms(_in_proj(h, win_ref, C_K, C_V), onesk_ref, kn_ref)
    kbuf[WINDOW:WINDOW + tile, :] = kn
    vbuf[WINDOW:WINDOW + tile, :] = _in_proj(h, win_ref, C_V, C_END)
    z = _in_proj(h, win_ref, C_Z, C_XBC)
    yield
    smask = smask_ref[...]
    dskip = dsk_ref[...]
    band = WINDOW + CHUNK
    ys, os_ = [], []
    for c in range(n_chunks):
        r = slice(c * CHUNK, (c + 1) * CHUNK)
        y_c, s_new = _ssd_segment(acs[r], xs[r], dt[r], bm[r], cm[r], s2_ref[...], dskip, smask,
                                  CHUNK)
        s2_ref[...] = s_new
        ys.append(y_c)
        yield
        first_valid = (WINDOW // CHUNK - (t * n_chunks + c)) * CHUNK
        os_.append(_attn_segment(qn[r], kbuf[c * CHUNK:c * CHUNK + band, :],
                                 vbuf[c * CHUNK:c * CHUNK + band, :], bias_ref, sink_ref,
                                 first_valid, CHUNK))
        yield
    y_ssd = _rms(jnp.concatenate(ys, axis=0) * _silu(z), ssdn_ref[...])
    o = jnp.concatenate(os_, axis=0)
    res["k_last"] = kbuf[tile:tile + WINDOW, :]
    res["v_last"] = vbuf[tile:tile + WINDOW, :]
    kbuf[0:WINDOW, :] = res["k_last"]
    vbuf[0:WINDOW, :] = res["v_last"]
    yield
    mix = (_dot(y_ssd.astype(BF16), wout_ref[0:D_SSD, :])
           + _dot(o.astype(BF16), wout_ref[D_SSD:D_SSD + D_ATTN, :]))
    res["x_mid"] = x + g1 * mix


def _layer_prompt_kernel(x_ref, modm_ref, modf_ref,
                         nmix_ref, win_ref, cw_ref, cb_ref, dtb_ref, alog_ref, dsk_ref, ssdn_ref,
                         qn_ref, kn_ref, sink_ref, bias_ref, tril_ref, onesq_ref, onesk_ref,
                         smask_ref, wout_ref,
                         nffn_ref, wup_ref, fcw_ref, fcb_ref, wdn_ref,
                         xo_ref, ssm_ref, ctail_ref, ko_ref, vo_ref, ftail_ref,
                         xbc_buf, kbuf, vbuf, s2_ref, xmid_buf, hist, *, tile, n_tiles):
    s = pl.program_id(0)
    t_mix = lax.rem(s, n_tiles)
    t_ffn = lax.rem(s + n_tiles - 1, n_tiles)
    slot_w = lax.rem(s, 2)
    slot_r = 1 - slot_w

    @pl.when(s == 0)
    def _():
        xmid_buf[...] = jnp.zeros(xmid_buf.shape, F32)

    @pl.when(t_mix == 0)
    def _():
        xbc_buf[...] = jnp.zeros((HIST, D_XBC), F32)
        kbuf[0:WINDOW, :] = jnp.zeros((WINDOW, D_KV), F32)
        vbuf[0:WINDOW, :] = jnp.zeros((WINDOW, D_KV), F32)
        s2_ref[...] = jnp.zeros((2 * D_STATE, D_SSD), F32)

    @pl.when(jnp.logical_or(t_ffn == 0, s == 0))
    def _():
        hist[...] = jnp.zeros(hist.shape, F32)

    mres, fres = {}, {}
    modf = modf_ref[0]
    mixer = _mixer_prompt_steps(
        mres, x_ref[0], modm_ref[0], t_mix, nmix_ref, win_ref, cw_ref, cb_ref, dtb_ref, alog_ref,
        dsk_ref, ssdn_ref, qn_ref, kn_ref, sink_ref, bias_ref, tril_ref, onesq_ref, onesk_ref,
        smask_ref, wout_ref, xbc_buf, kbuf, vbuf, s2_ref, tile)
    ffn = _ffn_steps(fres, xmid_buf[slot_r], modf[3:4], modf[4:5], modf[5:6], hist, nffn_ref,
                     wup_ref, fcw_ref, fcb_ref, wdn_ref, hist, None, None, 1, tile)
    for _ in range(N_FF_BLK + 1):
        next(ffn)
    _run_interleaved(mixer, ffn)
    xmid_buf[slot_w] = mres["x_mid"]
    xo_ref[0] = fres["out"]

    @pl.when(t_mix == n_tiles - 1)
    def _():
        ssm_ref[0] = _state_rows(s2_ref[...])
        ctail_ref[0] = mres["tail"]
        ko_ref[0] = mres["k_last"]
        vo_ref[0] = mres["v_last"]

    @pl.when(jnp.logical_and(t_ffn == n_tiles - 1, s > 0))
    def _():
        ftail_ref[...] = hist[...]


def _const_spec(shape):
    zeros = (0,) * len(shape)
    return pl.BlockSpec(shape, lambda *_: zeros, pipeline_mode=pl.Buffered(1))


def _layer_prompt_call(x, mod, lw, consts, tile):
    b, length, _ = x.shape
    n_tiles = length // tile
    n_steps = b * n_tiles
    band = WINDOW + CHUNK
    kern = functools.partial(_layer_prompt_kernel, tile=tile, n_tiles=n_tiles)

    def mix_tile(s):
        return jnp.minimum(s, n_steps - 1)

    def ffn_tile(s):
        return jnp.maximum(s - 1, 0)

    mixer_consts = (lw["norm_mix"], lw["w_in"], lw["conv_w"], lw["conv_b"], lw["dt_bias"],
                    lw["a_log"], lw["d_skip"], lw["ssd_norm"], lw["q_norm"], lw["k_norm"],
                    consts["sink_p"], consts["bias_p"], consts["tril_p"], consts["ones_q"],
                    consts["ones_k"], consts["smask"], lw["w_out"])
    ffn_consts = (lw["norm_ffn"], lw["w_up"], lw["conv_ffn_w"], lw["conv_ffn_b"], lw["w_down"])
    assert consts["bias_p"].shape == (2, band, 4 * CHUNK)
    return pl.pallas_call(
        kern,
        grid=(n_steps + 1,),
        in_specs=[
            pl.BlockSpec((1, tile, D_MODEL),
                         lambda s: (mix_tile(s) // n_tiles, mix_tile(s) % n_tiles, 0)),
            pl.BlockSpec((1, 8, D_MODEL), lambda s: (mix_tile(s) // n_tiles, 0, 0)),
            pl.BlockSpec((1, 8, D_MODEL), lambda s: (ffn_tile(s) // n_tiles, 0, 0)),
        ] + [_const_spec(a.shape) for a in mixer_consts + ffn_consts],
        out_specs=[
            pl.BlockSpec((1, tile, D_MODEL),
                         lambda s: (ffn_tile(s) // n_tiles, ffn_tile(s) % n_tiles, 0)),
            pl.BlockSpec((1, D_SSD, D_STATE), lambda s: (mix_tile(s) // n_tiles, 0, 0)),
            pl.BlockSpec((1, HIST, D_XBC), lambda s: (mix_tile(s) // n_tiles, 0, 0)),
            pl.BlockSpec((1, WINDOW, D_KV), lambda s: (mix_tile(s) // n_tiles, 0, 0)),
            pl.BlockSpec((1, WINDOW, D_KV), lambda s: (mix_tile(s) // n_tiles, 0, 0)),
            pl.BlockSpec((1, HIST, 2 * D_FF), lambda s: (ffn_tile(s) // n_tiles, 0, 0)),
        ],
        out_shape=[
            jax.ShapeDtypeStruct((b, length, D_MODEL), F32),
            jax.ShapeDtypeStruct((b, D_SSD, D_STATE), F32),
            jax.ShapeDtypeStruct((b, HIST, D_XBC), F32),
            jax.ShapeDtypeStruct((b, WINDOW, D_KV), F32),
            jax.ShapeDtypeStruct((b, WINDOW, D_KV), F32),
            jax.ShapeDtypeStruct((b, HIST, 2 * D_FF), F32),
        ],
        scratch_shapes=[
            pltpu.VMEM((HIST, D_XBC), F32),
            pltpu.VMEM((WINDOW + tile, D_KV), F32),
            pltpu.VMEM((WINDOW + tile, D_KV), F32),
            pltpu.VMEM((2 * D_STATE, D_SSD), F32),
            pltpu.VMEM((2, tile, D_MODEL), F32),
            pltpu.VMEM((1, HIST, 2 * D_FF), F32),
        ],
        compiler_params=pltpu.CompilerParams(
            dimension_semantics=("arbitrary",), vmem_limit_bytes=VMEM_LIMIT),
        name="layer_prompt",
    )(x, mod, mod, *mixer_consts, *ffn_consts)


def _mixer_sample_kernel(x_ref, sh_ref, sc_ref, g_ref, ssm_in_ref, ccache_ref, kc_ref, vc_ref,
                         nmix_ref, win_ref, cw_ref, cb_ref, dtb_ref, alog_ref,
                         dsk_ref, ssdn_ref, qn_ref, kn_ref, sink_ref, bias_ref, tril_ref,
                         onesq_ref, onesk_ref, smask_ref, wout_ref,
                         xo_ref, ssm_ref, ctail_ref, ko_ref, vo_ref,
                         xbc_buf, xs_buf, bc_buf, dt_buf, acs_buf, q_buf, k_buf, v_buf, y_buf, o_buf,
                         *, n_seq, length):
    rows = n_seq * length
    x = x_ref[...]
    h = (_rms(x, nmix_ref[...]) * (1.0 + sc_ref[...]) + sh_ref[...]).astype(BF16)

    xbc_raw = _in_proj(h, win_ref, C_XBC, C_DT)
    xbc_buf[:, 0:HIST, :] = ccache_ref[...]
    conv = _causal_conv(xbc_buf, xbc_raw.reshape(n_seq, length, D_XBC), cw_ref, cb_ref, SSD_CONV)
    ctail_ref[...] = xbc_buf[:, length:length + HIST, :]
    xbc = _silu(conv.reshape(rows, D_XBC))
    xs_buf[...] = xbc[:, 0:D_SSD]
    bc_buf[...] = xbc[:, D_SSD:D_XBC]
    dt = _softplus(_in_proj(h, win_ref, C_DT, C_Q) + dtb_ref[...])
    dt_buf[...] = dt
    acs_buf[...] = _dot_exact_rhs(tril_ref[...], dt * (-jnp.exp(alog_ref[...])), 3)
    q_buf[...] = _head_rms(_in_proj(h, win_ref, C_Q, C_K), onesq_ref, qn_ref)
    k_buf[...] = _head_rms(_in_proj(h, win_ref, C_K, C_V), onesk_ref, kn_ref)
    v_buf[...] = _in_proj(h, win_ref, C_V, C_END)

    def body(i, carry):
        r = pl.ds(pl.multiple_of(i * length, length), length)
        bc = bc_buf[r, :]
        y_i, s_new = _ssd_segment(acs_buf[r, :], xs_buf[r, :], dt_buf[r, :], bc[:, 0:D_BC],
                                  bc[:, D_BC:2 * D_BC], ssm_in_ref[i], dsk_ref[...],
                                  smask_ref[...], length)
        ssm_ref[i] = _state_rows(s_new)
        y_buf[r, :] = y_i
        kband = jnp.concatenate([kc_ref[i], k_buf[r, :]], axis=0)
        vband = jnp.concatenate([vc_ref[i], v_buf[r, :]], axis=0)
        ko_ref[i] = kband[length:length + WINDOW]
        vo_ref[i] = vband[length:length + WINDOW]
        o_buf[r, :] = _attn_segment(q_buf[r, :], kband, vband, bias_ref, sink_ref, None, length)
        return carry

    lax.fori_loop(0, n_seq, body, 0)

    z = _in_proj(h, win_ref, C_Z, C_XBC)
    y_ssd = _rms(y_buf[...] * _silu(z), ssdn_ref[...])
    mix = (_dot(y_ssd.astype(BF16), wout_ref[0:D_SSD, :])
           + _dot(o_buf[...].astype(BF16), wout_ref[D_SSD:D_SSD + D_ATTN, :]))
    xo_ref[...] = x + g_ref[...] * mix


def _lead_spec(shape):
    tail_zeros = (0,) * (len(shape) - 1)
    return pl.BlockSpec(shape, lambda i: (i,) + tail_zeros)


def _mixer_sample_call(x2, sh, sc, g, ssm_in, ccache, kc, vc, lw, consts, n_blk, length):
    n_seq = ssm_in.shape[0]
    rows = n_blk * length
    kern = functools.partial(_mixer_sample_kernel, n_seq=n_blk, length=length)
    row_args = (x2, sh, sc, g)
    seq_args = (ssm_in, ccache, kc, vc)
    const_args = (lw["norm_mix"], lw["w_in"], lw["conv_w"], lw["conv_b"], lw["dt_bias"],
                  lw["a_log"], lw["d_skip"], lw["ssd_norm"], lw["q_norm"], lw["k_norm"],
                  consts["sink_s"], consts["bias_s"], consts["tril_s"], consts["ones_q"],
                  consts["ones_k"], consts["smask"], lw["w_out"])
    out_shape = [
        jax.ShapeDtypeStruct((n_seq * length, D_MODEL), F32),
        jax.ShapeDtypeStruct((n_seq, D_SSD, D_STATE), F32),
        jax.ShapeDtypeStruct((n_seq, HIST, D_XBC), F32),
        jax.ShapeDtypeStruct((n_seq, WINDOW, D_KV), F32),
        jax.ShapeDtypeStruct((n_seq, WINDOW, D_KV), F32),
    ]
    return pl.pallas_call(
        kern,
        grid=(n_seq // n_blk,),
        in_specs=([_lead_spec((rows, D_MODEL)) for _ in row_args]
                  + [_lead_spec((n_blk,) + a.shape[1:]) for a in seq_args]
                  + [_const_spec(a.shape) for a in const_args]),
        out_specs=([_lead_spec((rows, D_MODEL))]
                   + [_lead_spec((n_blk,) + s.shape[1:]) for s in out_shape[1:]]),
        out_shape=out_shape,
        scratch_shapes=[
            pltpu.VMEM((n_blk, HIST + length, D_XBC), F32),
            pltpu.VMEM((rows, D_SSD), F32),
            pltpu.VMEM((rows, 2 * D_BC), F32),
            pltpu.VMEM((rows, D_SSD), F32),
            pltpu.VMEM((rows, D_SSD), F32),
            pltpu.VMEM((rows, D_ATTN), F32),
            pltpu.VMEM((rows, D_KV), F32),
            pltpu.VMEM((rows, D_KV), F32),
            pltpu.VMEM((rows, D_SSD), F32),
            pltpu.VMEM((rows, D_ATTN), F32),
        ],
        compiler_params=pltpu.CompilerParams(
            dimension_semantics=("arbitrary",), vmem_limit_bytes=VMEM_LIMIT),
        name="mixer_sample",
    )(*row_args, *seq_args, *const_args)


def _ffn_sample_kernel(x_ref, sh_ref, sc_ref, g_ref, hist_ref, nffn_ref, wup_ref, cw_ref, cb_ref,
                       wdn_ref, xo_ref, tail_ref, ubuf, gbuf, *, n_seq, length):
    res = {}
    for _ in _ffn_steps(res, x_ref[...], sh_ref[...], sc_ref[...], g_ref[...], hist_ref, nffn_ref,
                        wup_ref, cw_ref, cb_ref, wdn_ref, tail_ref, ubuf, gbuf, n_seq, length):
        pass
    xo_ref[...] = res["out"]


def _ffn_sample_call(x2, sh, sc, g, hist, lw, n_blk, length):
    n_seq = hist.shape[0]
    rows = n_blk * length
    kern = functools.partial(_ffn_sample_kernel, n_seq=n_blk, length=length)
    row_args = (x2, sh, sc, g)
    const_args = (lw["norm_ffn"], lw["w_up"], lw["conv_ffn_w"], lw["conv_ffn_b"], lw["w_down"])
    out_shape = [
        jax.ShapeDtypeStruct((n_seq * length, D_MODEL), F32),
        jax.ShapeDtypeStruct((n_seq, HIST, 2 * D_FF), F32),
    ]
    return pl.pallas_call(
        kern,
        grid=(n_seq // n_blk,),
        in_specs=([_lead_spec((rows, D_MODEL)) for _ in row_args]
                  + [_lead_spec((n_blk, HIST, 2 * D_FF))]
                  + [_const_spec(a.shape) for a in const_args]),
        out_specs=[_lead_spec((rows, D_MODEL)), _lead_spec((n_blk, HIST, 2 * D_FF))],
        out_shape=out_shape,
        scratch_shapes=[
            pltpu.VMEM((n_blk, HIST + length, FF_BLK), F32),
            pltpu.VMEM((n_blk, HIST + length, FF_BLK), F32),
        ],
        compiler_params=pltpu.CompilerParams(
            dimension_semantics=("arbitrary",), vmem_limit_bytes=VMEM_LIMIT),
        name="ffn_sample",
    )(*row_args, hist, *const_args)


def _expand_heads(v):
    return jnp.repeat(v.astype(F32), SSD_HEAD_DIM)[None, :]


def _prep_layer(l, w_in, conv_ssd_w, conv_ssd_b, dt_bias, a_log, d_skip, ssd_norm, q_norm, k_norm,
                w_out, norm_mix, norm_ffn, w_up, conv_ffn_w, conv_ffn_b, w_down):
    wi = w_in[l]
    o_xbc, o_dt = D_SSD, D_SSD + D_XBC
    o_q = o_dt + SSD_HEADS
    o_k, o_v = o_q + D_ATTN, o_q + D_ATTN + D_KV
    w_dt = jnp.repeat(wi[:, o_dt:o_q], SSD_HEAD_DIM, axis=1)
    w_in_r = jnp.concatenate(
        [wi[:, :o_xbc], wi[:, o_xbc:o_dt], w_dt, wi[:, o_q:o_k], wi[:, o_k:o_v], wi[:, o_v:]],
        axis=1).astype(BF16)
    return {
        "norm_mix": norm_mix[l][None, :],
        "w_in": w_in_r,
        "conv_w": conv_ssd_w[l],
        "conv_b": conv_ssd_b[l][None, :],
        "dt_bias": _expand_heads(dt_bias[l]),
        "a_log": _expand_heads(a_log[l]),
        "d_skip": _expand_heads(d_skip[l]),
        "ssd_norm": ssd_norm[l][None, :],
        "q_norm": jnp.tile(q_norm[l], ATTN_HEADS)[None, :],
        "k_norm": jnp.tile(k_norm[l], D_KV // HEAD_DIM)[None, :],
        "w_out": w_out[l].astype(BF16),
        "norm_ffn": norm_ffn[l][None, :],
        "w_up": w_up[l].astype(BF16),
        "conv_ffn_w": conv_ffn_w[l],
        "conv_ffn_b": conv_ffn_b[l][None, :],
        "w_down": w_down[l].astype(BF16),
    }


def _block_tril(n, blk):
    i = np.arange(n)
    return jnp.asarray(((i[:, None] // blk) == (i[None, :] // blk)) & (i[None, :] <= i[:, None]),
                       dtype=BF16)


def _block_ones(n, blk):
    i = np.arange(n)
    return jnp.asarray((i[:, None] // blk) == (i[None, :] // blk), dtype=BF16)


def _alibi_bias(q):
    slopes = (2.0 ** (-8.0 * np.arange(1, ATTN_HEADS + 1) / ATTN_HEADS)).astype(np.float32)
    dist = np.abs(np.arange(q)[:, None] + WINDOW - np.arange(WINDOW + q)[None, :]).astype(np.float32)
    bias = slopes[:, None, None] * dist[None]
    return jnp.asarray(np.transpose(bias.reshape(2, 4 * q, WINDOW + q), (0, 2, 1)))


def _sink_rows(sinks_l, q):
    return jnp.repeat(sinks_l.astype(F32), q).reshape(2, 1, 4 * q)


def _state_mask():
    r = np.arange(2 * D_STATE)[:, None] // D_STATE
    c = np.arange(D_SSD)[None, :] // (D_SSD // 2)
    return jnp.asarray(r == c, dtype=F32)


def _state_to_s2(state):
    n = state.shape[0]
    st = jnp.transpose(state, (0, 3, 1, 2)).reshape(n, D_STATE, D_SSD)
    half = D_SSD // 2
    zero = jnp.zeros((n, D_STATE, half), state.dtype)
    top = jnp.concatenate([st[:, :, :half], zero], axis=2)
    bot = jnp.concatenate([zero, st[:, :, half:]], axis=2)
    return jnp.concatenate([top, bot], axis=1)


def _pad_hist(cache):
    n, k, c = cache.shape
    return jnp.concatenate([jnp.zeros((n, HIST - k, c), cache.dtype), cache], axis=1)


def kernel(x_prompt, x_sample, c_prompt, c_sample, state_ssm, cache_conv_ssd, cache_attn_k,
           cache_attn_v, cache_conv_ffn, w_ada, b_ada, norm_mix, w_in, conv_ssd_w, conv_ssd_b,
           dt_bias, a_log, d_skip, ssd_norm, q_norm, k_norm, sinks, w_out, norm_ffn, w_up,
           conv_ffn_w, conv_ffn_b, w_down):
    depth = w_ada.shape[0]
    bp, seq, _ = x_prompt.shape
    bs, lsamp, _ = x_sample.shape
    rows_s = bs * lsamp
    tile = TILE_T
    n_blk = SAMPLE_SEQS

    mod_all = _ada_call(jnp.concatenate([c_prompt, c_sample], axis=0), w_ada, b_ada)
    mod_all = mod_all.reshape(depth, bp + bs, 6, D_MODEL)
    mod_p = jnp.pad(mod_all[:, :bp], ((0, 0), (0, 0), (0, 2), (0, 0)))
    mod_s = jnp.repeat(mod_all[:, bp:], lsamp, axis=1)

    consts = {
        "tril_p": _block_tril(tile, CHUNK),
        "tril_s": _block_tril(n_blk * lsamp, lsamp),
        "ones_q": _block_ones(D_ATTN, HEAD_DIM),
        "ones_k": _block_ones(D_KV, HEAD_DIM),
        "smask": _state_mask(),
        "bias_p": _alibi_bias(CHUNK),
        "bias_s": _alibi_bias(lsamp),
    }

    xp = x_prompt
    xs = x_sample.reshape(rows_s, D_MODEL)
    outs = [[] for _ in range(10)]
    for l in range(depth):
        lw = _prep_layer(l, w_in, conv_ssd_w, conv_ssd_b, dt_bias, a_log, d_skip, ssd_norm, q_norm,
                         k_norm, w_out, norm_mix, norm_ffn, w_up, conv_ffn_w, conv_ffn_b, w_down)
        consts["sink_p"] = _sink_rows(sinks[l], CHUNK)
        consts["sink_s"] = _sink_rows(sinks[l], lsamp)

        xp, ssm_p, ct_p, k_p, v_p, ft_p = _layer_prompt_call(xp, mod_p[l], lw, consts, tile)

        ms = mod_s[l]
        xs, ssm_s, ct_s, k_s, v_s = _mixer_sample_call(
            xs, ms[:, 0], ms[:, 1], ms[:, 2], _state_to_s2(state_ssm[l]),
            _pad_hist(cache_conv_ssd[l]), cache_attn_k[l].reshape(bs, WINDOW, D_KV),
            cache_attn_v[l].reshape(bs, WINDOW, D_KV), lw, consts, n_blk, lsamp)
        xs, ft_s = _ffn_sample_call(xs, ms[:, 3], ms[:, 4], ms[:, 5], _pad_hist(cache_conv_ffn[l]),
                                    lw, n_blk, lsamp)

        kv_shape = (WINDOW, D_KV // HEAD_DIM, HEAD_DIM)
        for lst, val in zip(outs, (
                ssm_p.reshape(bp, SSD_HEADS, SSD_HEAD_DIM, D_STATE),
                ssm_s.reshape(bs, SSD_HEADS, SSD_HEAD_DIM, D_STATE),
                ct_p[:, HIST - (SSD_CONV - 1):], ct_s[:, HIST - (SSD_CONV - 1):],
                k_p.reshape((bp,) + kv_shape), k_s.reshape((bs,) + kv_shape),
                v_p.reshape((bp,) + kv_shape), v_s.reshape((bs,) + kv_shape),
                ft_p[:, HIST - (FFN_CONV - 1):], ft_s[:, HIST - (FFN_CONV - 1):])):
            lst.append(val)

    return (xp, xs.reshape(bs, lsamp, D_MODEL)) + tuple(jnp.stack(o) for o in outs)
```

```python
import functools

import numpy as np
import jax
import jax.numpy as jnp
from jax import lax
from jax.experimental import pallas as pl
from jax.experimental.pallas import tpu as pltpu

F32 = jnp.float32
BF16 = jnp.bfloat16

D_MODEL = 1024
SSD_HEADS = 8
SSD_HEAD_DIM = 64
D_SSD = 512
D_STATE = 64
D_BC = 128
SSD_CONV = 4
D_XBC = 768
ATTN_HEADS = 8
HEAD_DIM = 64
D_ATTN = 512
D_KV = 128
WINDOW = 128
CHUNK = 64
D_FF = 2816
FFN_CONV = 3
EPS = 1e-6

C_Z, C_XBC, C_DT, C_Q, C_K, C_V, C_END = 0, 512, 1280, 1792, 2304, 2432, 2560

MXU_TILE = 256
HIST = 8
FF_BLK = 256
N_FF_BLK = D_FF // FF_BLK
STEP_ORDER = "f" * 12 + "m" * 4 + "mf" * 8
TILE_T = 256
SAMPLE_SEQS = 16
VMEM_LIMIT = 56 * 1024 * 1024


def _dot(a, b):
    return jnp.dot(a, b, preferred_element_type=F32)


def _dot_nt(a, b):
    return lax.dot_general(a, b, (((1,), (1,)), ((), ())), preferred_element_type=F32)


def _dot_tn(a, b):
    return lax.dot_general(a, b, (((0,), (0,)), ((), ())), preferred_element_type=F32)


def _split_bf16(x, parts):
    out = []
    r = x
    for i in range(parts):
        p = r.astype(BF16)
        out.append(p)
        if i + 1 < parts:
            r = r - p.astype(F32)
    return out


def _dot_exact_rhs(a_bf16, x, parts):
    acc = None
    for p in _split_bf16(x, parts):
        d = _dot(a_bf16, p)
        acc = d if acc is None else acc + d
    return acc


def _dot_exact_lhs(x, b_bf16, parts):
    acc = None
    for p in _split_bf16(x, parts):
        d = _dot(p, b_bf16)
        acc = d if acc is None else acc + d
    return acc


def _rms(x, g):
    ms = jnp.mean(x * x, axis=-1, keepdims=True)
    return x * lax.rsqrt(ms + EPS) * g


def _silu(x):
    return x * (1.0 / (1.0 + jnp.exp(-x)))


def _softplus(x):
    return jnp.maximum(x, 0.0) + jnp.log1p(jnp.exp(-jnp.abs(x)))


def _lane_lo(shape):
    return (lax.broadcasted_iota(jnp.int32, shape, len(shape) - 1) % 128) < 64


def _causal_conv(buf_ref, raw3, w_ref, b_ref, taps):
    length = raw3.shape[1]
    buf_ref[:, HIST:HIST + length, :] = raw3
    y = b_ref[...]
    for k in range(taps):
        lo = HIST - (taps - 1) + k
        y = y + buf_ref[:, lo:lo + length, :] * w_ref[k:k + 1, :]
    return y


def _causal_conv_rolled(raw, hist, w_ref, b_ref, taps):
    row = lax.broadcasted_iota(jnp.int32, hist.shape, 0)
    y = b_ref[...]
    for k in range(taps):
        s = taps - 1 - k
        if s == 0:
            term = raw
        else:
            rolled = pltpu.roll(raw, s, axis=0)
            head = jnp.where(row < s, pltpu.roll(hist, s, axis=0), rolled[0:HIST])
            term = jnp.concatenate([head, rolled[HIST:]], axis=0)
        y = y + term * w_ref[k:k + 1, :]
    return y


def _ssd_steps(res, key, acs, xs, dt, bm, cm, state, dskip, smask, q):
    pad = CHUNK - q

    def stack2(a, b):
        if pad:
            zero = jnp.zeros((pad, a.shape[1]), a.dtype)
            return jnp.concatenate([a, zero, b, zero], axis=0)
        return jnp.concatenate([a, b], axis=0)

    lane = lax.broadcasted_iota(jnp.int32, (q, D_SSD), 1)
    row = lax.broadcasted_iota(jnp.int32, (q, D_SSD), 0)
    rowf = jnp.sum(jnp.where((lane % CHUNK) == row, acs, 0.0), axis=0, keepdims=True)
    a_end = acs[q - 1:q, :]
    xdt = xs * dt
    xw = (xdt * jnp.exp(a_end - acs)).astype(BF16)
    cm_b = cm.astype(BF16)
    zrow = jnp.zeros((2 * D_STATE - q, D_BC), F32)
    bm_t = jnp.concatenate([bm, zrow], axis=0).T.astype(BF16)
    xw_pad = jnp.concatenate([xw, jnp.zeros((2 * D_STATE - q, D_SSD), BF16)], axis=0)
    lo128 = _lane_lo((q, 128))
    bstack = stack2(bm, bm).astype(BF16)
    yield
    s_add = _dot(bm_t, xw_pad) * smask
    cbs = [_dot_nt(jnp.where(lo128 if g == 0 else ~lo128, cm, 0.0).astype(BF16), bstack)
           for g in range(2)]
    yield
    lane128 = lax.broadcasted_iota(jnp.int32, (q, 128), 1)
    row128 = lax.broadcasted_iota(jnp.int32, (q, 128), 0)
    causal = (lane128 % CHUNK) <= row128
    ms, rhss = [], []
    for j in range(4):
        seg = acs[:, 128 * j:128 * (j + 1)] - rowf[:, 128 * j:128 * (j + 1)]
        decay = jnp.exp(jnp.where(causal, seg, -jnp.inf))
        ms.append((cbs[j // 2] * decay).astype(BF16))
        xp = xdt[:, 128 * j:128 * (j + 1)]
        rhss.append(stack2(jnp.where(lo128, xp, 0.0), jnp.where(lo128, 0.0, xp)).astype(BF16))
    yield
    s2 = state["s2"]
    y_off = _dot(cm_b, s2.astype(BF16)) * jnp.exp(acs)
    state["s2"] = s2 * jnp.exp(a_end) + s_add
    y_diag = jnp.concatenate([_dot(m, r) for m, r in zip(ms, rhss)], axis=1)
    res[key] = y_diag + y_off + dskip * xs


def _dup_halves(x):
    lo = _lane_lo(x.shape)
    sw = pltpu.roll(x, 64, axis=1)
    return jnp.where(lo, x, sw).astype(BF16), jnp.where(lo, sw, x).astype(BF16)


def _attn_steps(res, key, qg, kd, vd, bias, sink, first_valid, q):
    band = kd.shape[0]
    lo_q = _lane_lo((q, 128))
    p0, p1 = qg[:, 0:128], qg[:, 128:256]
    qs = jnp.concatenate([jnp.where(lo_q, p0, 0.0), jnp.where(lo_q, 0.0, p0),
                          jnp.where(lo_q, p1, 0.0), jnp.where(lo_q, 0.0, p1)], axis=0)
    s = _dot_nt(kd, qs.astype(BF16))
    yield
    s = s * (HEAD_DIM ** -0.5) - bias
    if first_valid is not None:
        s = jnp.where(lax.broadcasted_iota(jnp.int32, (band, 1), 0) >= first_valid, s, -jnp.inf)
    m = jnp.maximum(jnp.max(s, axis=0, keepdims=True), sink)
    p = jnp.exp(s - m)
    p = (p / (jnp.sum(p, axis=0, keepdims=True) + jnp.exp(sink - m))).astype(BF16)
    yield
    og = _dot_tn(p, vd)
    yield
    res[key] = jnp.concatenate([jnp.where(lo_q, og[0:q], og[q:2 * q]),
                                jnp.where(lo_q, og[2 * q:3 * q], og[3 * q:4 * q])], axis=1)


def _run_staggered(gens):
    live, started = [], 0
    while started < len(gens) or live:
        if started < len(gens):
            live.append(gens[started])
            started += 1
        for g in list(live):
            try:
                next(g)
            except StopIteration:
                live.remove(g)
        yield


def _drain(gen):
    for _ in gen:
        pass


def _state_rows(s2):
    s_t = s2.T
    half = D_SSD // 2
    return jnp.concatenate([s_t[0:half, 0:D_STATE], s_t[half:D_SSD, D_STATE:2 * D_STATE]], axis=0)


def _head_rms(x, ones_ref, g_ref):
    w = ones_ref.shape[0]
    ones = ones_ref[...]
    sq = x * x
    ssq = jnp.concatenate([_dot_exact_lhs(sq[:, i:i + w], ones, 2)
                           for i in range(0, x.shape[1], w)], axis=1)
    return x * lax.rsqrt(ssq * (1.0 / HEAD_DIM) + EPS) * g_ref[...]


def _in_proj(h, win_ref, lo, hi):
    return _dot(h, win_ref[:, lo:hi])


def _run_ordered(order, gens):
    for who in order:
        next(gens[who], None)
    for g in gens.values():
        _drain(g)


def _ada_kernel(c_ref, w_ref, b_ref, o_ref):
    c = c_ref[...]
    o_ref[0] = _dot(_silu(c).astype(BF16), w_ref[0].astype(BF16)) + b_ref[0]


def _ada_call(c_all, w_ada, b_ada):
    depth = w_ada.shape[0]
    n = c_all.shape[0]
    blk = 1024
    return pl.pallas_call(
        _ada_kernel,
        grid=(depth, 6 * D_MODEL // blk),
        in_specs=[
            pl.BlockSpec((n, D_MODEL), lambda l, j: (0, 0)),
            pl.BlockSpec((1, D_MODEL, blk), lambda l, j: (l, 0, j)),
            pl.BlockSpec((1, 1, blk), lambda l, j: (l, 0, j)),
        ],
        out_specs=pl.BlockSpec((1, n, blk), lambda l, j: (l, 0, j)),
        out_shape=jax.ShapeDtypeStruct((depth, n, 6 * D_MODEL), F32),
        compiler_params=pltpu.CompilerParams(
            dimension_semantics=("arbitrary", "arbitrary"), vmem_limit_bytes=VMEM_LIMIT),
        name="adaln",
    )(c_all, w_ada, b_ada.reshape(depth, 1, 6 * D_MODEL))


def _ffn_norm(x, sh, sc, nffn_ref):
    return (_rms(x, nffn_ref[...]) * (1.0 + sc) + sh).astype(BF16)


def _ffn_steps(out_ref, x, h, g, hist_ref, wup_ref, cw_ref, cb_ref, wdn_ref, tail_ref,
               ubuf, gbuf, n_seq, length):
    rows = n_seq * length

    def up(j):
        return [_dot(h, wup_ref[:, base + j * FF_BLK:base + (j + 1) * FF_BLK]) for base in (0, D_FF)]

    acts = []
    raws = up(0)
    yield
    for j in range(N_FF_BLK):
        nxt = up(j + 1) if j + 1 < N_FF_BLK else None
        halves = []
        for raw, buf, base in zip(raws, (ubuf, gbuf), (0, D_FF)):
            lo = base + j * FF_BLK
            cw, cb = cw_ref.at[:, lo:lo + FF_BLK], cb_ref.at[:, lo:lo + FF_BLK]
            if n_seq == 1:
                y = _causal_conv_rolled(raw, hist_ref[0, :, lo:lo + FF_BLK], cw, cb, FFN_CONV)
                tail_ref[0, :, lo:lo + FF_BLK] = raw[length - HIST:length]
            else:
                buf[:, 0:HIST, :] = hist_ref[:, :, lo:lo + FF_BLK]
                y = _causal_conv(buf, raw.reshape(n_seq, length, FF_BLK), cw, cb, FFN_CONV)
                tail_ref[:, :, lo:lo + FF_BLK] = buf[:, length:length + HIST, :]
            halves.append(y.reshape(rows, FF_BLK))
        acts.append((_silu(halves[1]) * halves[0]).astype(BF16))
        raws = nxt
        yield
    k_split = (N_FF_BLK // 2 + 1) * FF_BLK
    act_a = jnp.concatenate(acts[:N_FF_BLK // 2 + 1], axis=1)
    act_b = jnp.concatenate(acts[N_FF_BLK // 2 + 1:], axis=1)
    for n in range(D_MODEL // FF_BLK):
        cols = slice(n * FF_BLK, (n + 1) * FF_BLK)
        d = _dot(act_a, wdn_ref[0:k_split, cols])
        yield
        d = d + _dot(act_b, wdn_ref[k_split:D_FF, cols])
        out_ref[:, cols] = x[:, cols] + g[:, cols] * d
        yield


def _mixer_prompt_steps(res, x, mod, t, nmix_ref, win_ref, cw_ref, cb_ref, dtb_ref, alog_ref,
                        dsk_ref, ssdn_ref, qn_ref, kn_ref, sink_ref, bias_ref, tril_ref, onesq_ref,
                        onesk_ref, smask_ref, wout_ref, xbc_buf, kbuf, vbuf, s2_ref, tile):
    n_chunks = tile // CHUNK
    sh1, sc1, g1 = mod[0:1], mod[1:2], mod[2:3]
    h = (_rms(x, nmix_ref[...]) * (1.0 + sc1) + sh1).astype(BF16)

    xbc_raw = _in_proj(h, win_ref, C_XBC, C_DT)
    yield
    conv = _causal_conv_rolled(xbc_raw, xbc_buf[...], cw_ref, cb_ref, SSD_CONV)
    res["tail"] = xbc_raw[tile - HIST:tile]
    xbc_buf[...] = res["tail"]
    xbc = _silu(conv)
    xs, bm, cm = xbc[:, 0:D_SSD], xbc[:, D_SSD:D_SSD + D_BC], xbc[:, D_SSD + D_BC:D_XBC]
    yield
    dt = _softplus(_in_proj(h, win_ref, C_DT, C_Q) + dtb_ref[...])
    acs = _dot_exact_rhs(tril_ref[...], dt * (-jnp.exp(alog_ref[...])), 3)
    yield
    qn = _head_rms(_in_proj(h, win_ref, C_Q, C_K), onesq_ref, qn_ref)
    kn = _head_rms(_in_proj(h, win_ref, C_K, C_V), onesk_ref, kn_ref)
    vn = _in_proj(h, win_ref, C_V, C_END)
    z = _in_proj(h, win_ref, C_Z, C_XBC)
    yield
    smask = smask_ref[...]
    dskip = dsk_ref[...]
    band = WINDOW + CHUNK
    kds = _dup_halves(jnp.concatenate([kbuf[...], kn], axis=0))
    vds = _dup_halves(jnp.concatenate([vbuf[...], vn], axis=0))
    res["k_last"] = kn[tile - WINDOW:tile]
    res["v_last"] = vn[tile - WINDOW:tile]
    kbuf[...] = res["k_last"]
    vbuf[...] = res["v_last"]
    state = {"s2": s2_ref[...]}
    part = {}
    items = []
    for c in range(n_chunks):
        r = slice(c * CHUNK, (c + 1) * CHUNK)
        kr = slice(c * CHUNK, c * CHUNK + band)
        first_valid = (WINDOW // CHUNK - (t * n_chunks + c)) * CHUNK
        items.append(_ssd_steps(part, ("y", c), acs[r], xs[r], dt[r], bm[r], cm[r], state, dskip,
                                smask, CHUNK))
        for g in range(2):
            items.append(_attn_steps(part, ("o", c, g), qn[r, 256 * g:256 * (g + 1)], kds[g][kr],
                                     vds[g][kr], bias_ref[g], sink_ref[g], first_valid, CHUNK))
    yield from _run_staggered(items)
    s2_ref[...] = state["s2"]
    y_ssd = _rms(jnp.concatenate([part["y", c] for c in range(n_chunks)], axis=0) * _silu(z),
                 ssdn_ref[...])
    o = jnp.concatenate([jnp.concatenate([part["o", c, 0], part["o", c, 1]], axis=1)
                         for c in range(n_chunks)], axis=0)
    yield
    mix = (_dot(y_ssd.astype(BF16), wout_ref[0:D_SSD, :])
           + _dot(o.astype(BF16), wout_ref[D_SSD:D_SSD + D_ATTN, :]))
    res["x_mid"] = x + g1 * mix


def _layer_prompt_kernel(x_ref, modm_ref, modf_ref,
                         nmix_ref, win_ref, cw_ref, cb_ref, dtb_ref, alog_ref, dsk_ref, ssdn_ref,
                         qn_ref, kn_ref, sink_ref, bias_ref, tril_ref, onesq_ref, onesk_ref,
                         smask_ref, wout_ref,
                         nffn_ref, wup_ref, fcw_ref, fcb_ref, wdn_ref,
                         xo_ref, ssm_ref, ctail_ref, ko_ref, vo_ref, ftail_ref,
                         xbc_buf, kbuf, vbuf, s2_ref, xmid_buf, hffn_buf, hist, *, tile, n_tiles):
    s = pl.program_id(0)
    t_mix = lax.rem(s, n_tiles)
    t_ffn = lax.rem(s + n_tiles - 1, n_tiles)
    slot_w = lax.rem(s, 2)
    slot_r = 1 - slot_w

    @pl.when(s == 0)
    def _():
        xmid_buf[...] = jnp.zeros(xmid_buf.shape, F32)
        hffn_buf[...] = jnp.zeros(hffn_buf.shape, BF16)

    @pl.when(t_mix == 0)
    def _():
        xbc_buf[...] = jnp.zeros((HIST, D_XBC), F32)
        kbuf[...] = jnp.zeros((WINDOW, D_KV), F32)
        vbuf[...] = jnp.zeros((WINDOW, D_KV), F32)
        s2_ref[...] = jnp.zeros((2 * D_STATE, D_SSD), F32)

    @pl.when(jnp.logical_or(t_ffn == 0, s == 0))
    def _():
        hist[...] = jnp.zeros(hist.shape, F32)

    mres = {}
    modm = modm_ref[0]
    mixer = _mixer_prompt_steps(
        mres, x_ref[0], modm, t_mix, nmix_ref, win_ref, cw_ref, cb_ref, dtb_ref, alog_ref,
        dsk_ref, ssdn_ref, qn_ref, kn_ref, sink_ref, bias_ref, tril_ref, onesq_ref, onesk_ref,
        smask_ref, wout_ref, xbc_buf, kbuf, vbuf, s2_ref, tile)
    ffn = _ffn_steps(xo_ref.at[0], xmid_buf[slot_r], hffn_buf[slot_r], modf_ref[0, 5:6, :], hist,
                     wup_ref, fcw_ref, fcb_ref, wdn_ref, hist, None, None, 1, tile)
    def mixer_then_norm():
        yield from mixer
        yield
        xmid_buf[slot_w] = mres["x_mid"]
        hffn_buf[slot_w] = _ffn_norm(mres["x_mid"], modm[3:4], modm[4:5], nffn_ref)

    _run_ordered(STEP_ORDER, {"f": ffn, "m": mixer_then_norm()})

    @pl.when(t_mix == n_tiles - 1)
    def _():
        ssm_ref[0] = _state_rows(s2_ref[...])
        ctail_ref[0] = mres["tail"]
        ko_ref[0] = mres["k_last"]
        vo_ref[0] = mres["v_last"]

    @pl.when(jnp.logical_and(t_ffn == n_tiles - 1, s > 0))
    def _():
        ftail_ref[...] = hist[...]


def _const_spec(shape):
    zeros = (0,) * len(shape)
    return pl.BlockSpec(shape, lambda *_: zeros, pipeline_mode=pl.Buffered(1))


def _layer_prompt_call(x, mod, lw, consts, tile):
    b, length, _ = x.shape
    n_tiles = length // tile
    n_steps = b * n_tiles
    band = WINDOW + CHUNK
    kern = functools.partial(_layer_prompt_kernel, tile=tile, n_tiles=n_tiles)

    def mix_tile(s):
        return jnp.minimum(s, n_steps - 1)

    def ffn_tile(s):
        return jnp.maximum(s - 1, 0)

    mixer_consts = (lw["norm_mix"], lw["w_in"], lw["conv_w"], lw["conv_b"], lw["dt_bias"],
                    lw["a_log"], lw["d_skip"], lw["ssd_norm"], lw["q_norm"], lw["k_norm"],
                    consts["sink_p"], consts["bias_p"], consts["tril_p"], consts["ones_q"],
                    consts["ones_k"], consts["smask"], lw["w_out"])
    ffn_consts = (lw["norm_ffn"], lw["w_up"], lw["conv_ffn_w"], lw["conv_ffn_b"], lw["w_down"])
    assert consts["bias_p"].shape == (2, band, 4 * CHUNK)
    return pl.pallas_call(
        kern,
        grid=(n_steps + 1,),
        in_specs=[
            pl.BlockSpec((1, tile, D_MODEL),
                         lambda s: (mix_tile(s) // n_tiles, mix_tile(s) % n_tiles, 0)),
            pl.BlockSpec((1, 8, D_MODEL), lambda s: (mix_tile(s) // n_tiles, 0, 0)),
            pl.BlockSpec((1, 8, D_MODEL), lambda s: (ffn_tile(s) // n_tiles, 0, 0)),
        ] + [_const_spec(a.shape) for a in mixer_consts + ffn_consts],
        out_specs=[
            pl.BlockSpec((1, tile, D_MODEL),
                         lambda s: (ffn_tile(s) // n_tiles, ffn_tile(s) % n_tiles, 0)),
            pl.BlockSpec((1, D_SSD, D_STATE), lambda s: (mix_tile(s) // n_tiles, 0, 0)),
            pl.BlockSpec((1, HIST, D_XBC), lambda s: (mix_tile(s) // n_tiles, 0, 0)),
            pl.BlockSpec((1, WINDOW, D_KV), lambda s: (mix_tile(s) // n_tiles, 0, 0)),
            pl.BlockSpec((1, WINDOW, D_KV), lambda s: (mix_tile(s) // n_tiles, 0, 0)),
            pl.BlockSpec((1, HIST, 2 * D_FF), lambda s: (ffn_tile(s) // n_tiles, 0, 0)),
        ],
        out_shape=[
            jax.ShapeDtypeStruct((b, length, D_MODEL), F32),
            jax.ShapeDtypeStruct((b, D_SSD, D_STATE), F32),
            jax.ShapeDtypeStruct((b, HIST, D_XBC), F32),
            jax.ShapeDtypeStruct((b, WINDOW, D_KV), F32),
            jax.ShapeDtypeStruct((b, WINDOW, D_KV), F32),
            jax.ShapeDtypeStruct((b, HIST, 2 * D_FF), F32),
        ],
        scratch_shapes=[
            pltpu.VMEM((HIST, D_XBC), F32),
            pltpu.VMEM((WINDOW, D_KV), F32),
            pltpu.VMEM((WINDOW, D_KV), F32),
            pltpu.VMEM((2 * D_STATE, D_SSD), F32),
            pltpu.VMEM((2, tile, D_MODEL), F32),
            pltpu.VMEM((2, tile, D_MODEL), BF16),
            pltpu.VMEM((1, HIST, 2 * D_FF), F32),
        ],
        compiler_params=pltpu.CompilerParams(
            dimension_semantics=("arbitrary",), vmem_limit_bytes=VMEM_LIMIT),
        name="layer_prompt",
    )(x, mod, mod, *mixer_consts, *ffn_consts)


def _seq_rows(mod_ref, k, length):
    v = mod_ref[k]
    n_seq, d = v.shape
    return jnp.broadcast_to(v[:, None, :], (n_seq, length, d)).reshape(n_seq * length, d)


def _mixer_sample_kernel(x_ref, mod_ref, ssm_in_ref, ccache_ref, kc_ref, vc_ref,
                         nmix_ref, win_ref, cw_ref, cb_ref, dtb_ref, alog_ref,
                         dsk_ref, ssdn_ref, qn_ref, kn_ref, sink_ref, bias_ref, tril_ref,
                         onesq_ref, onesk_ref, smask_ref, wout_ref,
                         xo_ref, ssm_ref, ctail_ref, ko_ref, vo_ref,
                         xbc_buf, xs_buf, bc_buf, dt_buf, acs_buf, q_buf, k_buf, v_buf, y_buf, o_buf,
                         *, n_seq, length):
    rows = n_seq * length
    x = x_ref[...]
    h = (_rms(x, nmix_ref[...]) * (1.0 + _seq_rows(mod_ref, 1, length))
         + _seq_rows(mod_ref, 0, length)).astype(BF16)

    xbc_raw = _in_proj(h, win_ref, C_XBC, C_DT)
    xbc_buf[:, 0:HIST, :] = ccache_ref[...]
    conv = _causal_conv(xbc_buf, xbc_raw.reshape(n_seq, length, D_XBC), cw_ref, cb_ref, SSD_CONV)
    ctail_ref[...] = xbc_buf[:, length:length + HIST, :]
    xbc = _silu(conv.reshape(rows, D_XBC))
    xs_buf[...] = xbc[:, 0:D_SSD]
    bc_buf[...] = xbc[:, D_SSD:D_XBC]
    dt = _softplus(_in_proj(h, win_ref, C_DT, C_Q) + dtb_ref[...])
    dt_buf[...] = dt
    acs_buf[...] = _dot_exact_rhs(tril_ref[...], dt * (-jnp.exp(alog_ref[...])), 3)
    q_buf[...] = _head_rms(_in_proj(h, win_ref, C_Q, C_K), onesq_ref, qn_ref)
    k_buf[...] = _head_rms(_in_proj(h, win_ref, C_K, C_V), onesk_ref, kn_ref)
    v_buf[...] = _in_proj(h, win_ref, C_V, C_END)

    def body(i, carry):
        r = pl.ds(pl.multiple_of(i * length, length), length)
        bc = bc_buf[r, :]
        part = {}
        state = {"s2": ssm_in_ref[i]}
        _drain(_ssd_steps(part, "y", acs_buf[r, :], xs_buf[r, :], dt_buf[r, :], bc[:, 0:D_BC],
                          bc[:, D_BC:2 * D_BC], state, dsk_ref[...], smask_ref[...], length))
        ssm_ref[i] = _state_rows(state["s2"])
        y_buf[r, :] = part["y"]
        kband = jnp.concatenate([kc_ref[i], k_buf[r, :]], axis=0)
        vband = jnp.concatenate([vc_ref[i], v_buf[r, :]], axis=0)
        ko_ref[i] = kband[length:length + WINDOW]
        vo_ref[i] = vband[length:length + WINDOW]
        kds, vds = _dup_halves(kband), _dup_halves(vband)
        q_i = q_buf[r, :]
        for g in range(2):
            _drain(_attn_steps(part, ("o", g), q_i[:, 256 * g:256 * (g + 1)], kds[g], vds[g],
                               bias_ref[g], sink_ref[g], None, length))
        o_buf[r, :] = jnp.concatenate([part["o", 0], part["o", 1]], axis=1)
        return carry

    lax.fori_loop(0, n_seq, body, 0)

    z = _in_proj(h, win_ref, C_Z, C_XBC)
    y_ssd = _rms(y_buf[...] * _silu(z), ssdn_ref[...])
    mix = (_dot(y_ssd.astype(BF16), wout_ref[0:D_SSD, :])
           + _dot(o_buf[...].astype(BF16), wout_ref[D_SSD:D_SSD + D_ATTN, :]))
    xo_ref[...] = x + _seq_rows(mod_ref, 2, length) * mix


def _lead_spec(shape):
    tail_zeros = (0,) * (len(shape) - 1)
    return pl.BlockSpec(shape, lambda i: (i,) + tail_zeros)


def _mod_spec(n_blk):
    return pl.BlockSpec((6, n_blk, D_MODEL), lambda i: (0, i, 0))


def _mixer_sample_call(x2, mod, ssm_in, ccache, kc, vc, lw, consts, n_blk, length):
    n_seq = ssm_in.shape[0]
    rows = n_blk * length
    kern = functools.partial(_mixer_sample_kernel, n_seq=n_blk, length=length)
    seq_args = (ssm_in, ccache, kc, vc)
    const_args = (lw["norm_mix"], lw["w_in"], lw["conv_w"], lw["conv_b"], lw["dt_bias"],
                  lw["a_log"], lw["d_skip"], lw["ssd_norm"], lw["q_norm"], lw["k_norm"],
                  consts["sink_s"], consts["bias_s"], consts["tril_s"], consts["ones_q"],
                  consts["ones_k"], consts["smask"], lw["w_out"])
    out_shape = [
        jax.ShapeDtypeStruct((n_seq * length, D_MODEL), F32),
        jax.ShapeDtypeStruct((n_seq, D_SSD, D_STATE), F32),
        jax.ShapeDtypeStruct((n_seq, HIST, D_XBC), F32),
        jax.ShapeDtypeStruct((n_seq, WINDOW, D_KV), F32),
        jax.ShapeDtypeStruct((n_seq, WINDOW, D_KV), F32),
    ]
    return pl.pallas_call(
        kern,
        grid=(n_seq // n_blk,),
        in_specs=([_lead_spec((rows, D_MODEL)), _mod_spec(n_blk)]
                  + [_lead_spec((n_blk,) + a.shape[1:]) for a in seq_args]
                  + [_const_spec(a.shape) for a in const_args]),
        out_specs=([_lead_spec((rows, D_MODEL))]
                   + [_lead_spec((n_blk,) + s.shape[1:]) for s in out_shape[1:]]),
        out_shape=out_shape,
        scratch_shapes=[
            pltpu.VMEM((n_blk, HIST + length, D_XBC), F32),
            pltpu.VMEM((rows, D_SSD), F32),
            pltpu.VMEM((rows, 2 * D_BC), F32),
            pltpu.VMEM((rows, D_SSD), F32),
            pltpu.VMEM((rows, D_SSD), F32),
            pltpu.VMEM((rows, D_ATTN), F32),
            pltpu.VMEM((rows, D_KV), F32),
            pltpu.VMEM((rows, D_KV), F32),
            pltpu.VMEM((rows, D_SSD), F32),
            pltpu.VMEM((rows, D_ATTN), F32),
        ],
        compiler_params=pltpu.CompilerParams(
            dimension_semantics=("arbitrary",), vmem_limit_bytes=VMEM_LIMIT),
        name="mixer_sample",
    )(x2, mod, *seq_args, *const_args)


def _ffn_sample_kernel(x_ref, mod_ref, hist_ref, nffn_ref, wup_ref, cw_ref, cb_ref,
                       wdn_ref, xo_ref, tail_ref, ubuf, gbuf, *, n_seq, length):
    x = x_ref[...]
    h = _ffn_norm(x, _seq_rows(mod_ref, 3, length), _seq_rows(mod_ref, 4, length), nffn_ref)
    _drain(_ffn_steps(xo_ref, x, h, _seq_rows(mod_ref, 5, length), hist_ref, wup_ref, cw_ref,
                      cb_ref, wdn_ref, tail_ref, ubuf, gbuf, n_seq, length))


def _ffn_sample_call(x2, mod, hist, lw, n_blk, length):
    n_seq = hist.shape[0]
    rows = n_blk * length
    kern = functools.partial(_ffn_sample_kernel, n_seq=n_blk, length=length)
    const_args = (lw["norm_ffn"], lw["w_up"], lw["conv_ffn_w"], lw["conv_ffn_b"], lw["w_down"])
    out_shape = [
        jax.ShapeDtypeStruct((n_seq * length, D_MODEL), F32),
        jax.ShapeDtypeStruct((n_seq, HIST, 2 * D_FF), F32),
    ]
    return pl.pallas_call(
        kern,
        grid=(n_seq // n_blk,),
        in_specs=([_lead_spec((rows, D_MODEL)), _mod_spec(n_blk)]
                  + [_lead_spec((n_blk, HIST, 2 * D_FF))]
                  + [_const_spec(a.shape) for a in const_args]),
        out_specs=[_lead_spec((rows, D_MODEL)), _lead_spec((n_blk, HIST, 2 * D_FF))],
        out_shape=out_shape,
        scratch_shapes=[
            pltpu.VMEM((n_blk, HIST + length, FF_BLK), F32),
            pltpu.VMEM((n_blk, HIST + length, FF_BLK), F32),
        ],
        compiler_params=pltpu.CompilerParams(
            dimension_semantics=("arbitrary",), vmem_limit_bytes=VMEM_LIMIT),
        name="ffn_sample",
    )(x2, mod, hist, *const_args)


def _expand_heads(v):
    return jnp.repeat(v.astype(F32), SSD_HEAD_DIM)[None, :]


def _prep_layer(l, w_in, conv_ssd_w, conv_ssd_b, dt_bias, a_log, d_skip, ssd_norm, q_norm, k_norm,
                w_out, norm_mix, norm_ffn, w_up, conv_ffn_w, conv_ffn_b, w_down):
    wi = w_in[l]
    o_xbc, o_dt = D_SSD, D_SSD + D_XBC
    o_q = o_dt + SSD_HEADS
    o_k, o_v = o_q + D_ATTN, o_q + D_ATTN + D_KV
    w_dt = jnp.repeat(wi[:, o_dt:o_q], SSD_HEAD_DIM, axis=1)
    w_in_r = jnp.concatenate(
        [wi[:, :o_xbc], wi[:, o_xbc:o_dt], w_dt, wi[:, o_q:o_k], wi[:, o_k:o_v], wi[:, o_v:]],
        axis=1).astype(BF16)
    return {
        "norm_mix": norm_mix[l][None, :],
        "w_in": w_in_r,
        "conv_w": conv_ssd_w[l],
        "conv_b": conv_ssd_b[l][None, :],
        "dt_bias": _expand_heads(dt_bias[l]),
        "a_log": _expand_heads(a_log[l]),
        "d_skip": _expand_heads(d_skip[l]),
        "ssd_norm": ssd_norm[l][None, :],
        "q_norm": jnp.tile(q_norm[l], ATTN_HEADS)[None, :],
        "k_norm": jnp.tile(k_norm[l], D_KV // HEAD_DIM)[None, :],
        "w_out": w_out[l].astype(BF16),
        "norm_ffn": norm_ffn[l][None, :],
        "w_up": w_up[l].astype(BF16),
        "conv_ffn_w": conv_ffn_w[l],
        "conv_ffn_b": conv_ffn_b[l][None, :],
        "w_down": w_down[l].astype(BF16),
    }


def _block_tril(n, blk):
    i = np.arange(n)
    return jnp.asarray(((i[:, None] // blk) == (i[None, :] // blk)) & (i[None, :] <= i[:, None]),
                       dtype=BF16)


def _block_ones(n, blk):
    i = np.arange(n)
    return jnp.asarray((i[:, None] // blk) == (i[None, :] // blk), dtype=BF16)


def _alibi_bias(q):
    slopes = (2.0 ** (-8.0 * np.arange(1, ATTN_HEADS + 1) / ATTN_HEADS)).astype(np.float32)
    dist = np.abs(np.arange(q)[:, None] + WINDOW - np.arange(WINDOW + q)[None, :]).astype(np.float32)
    bias = slopes[:, None, None] * dist[None]
    return jnp.asarray(np.transpose(bias.reshape(2, 4 * q, WINDOW + q), (0, 2, 1)))


def _sink_rows(sinks_l, q):
    return jnp.repeat(sinks_l.astype(F32), q).reshape(2, 1, 4 * q)


def _state_mask():
    r = np.arange(2 * D_STATE)[:, None] // D_STATE
    c = np.arange(D_SSD)[None, :] // (D_SSD // 2)
    return jnp.asarray(r == c, dtype=F32)


def _state_to_s2(state):
    n = state.shape[0]
    st = jnp.transpose(state, (0, 3, 1, 2)).reshape(n, D_STATE, D_SSD)
    half = D_SSD // 2
    zero = jnp.zeros((n, D_STATE, half), state.dtype)
    top = jnp.concatenate([st[:, :, :half], zero], axis=2)
    bot = jnp.concatenate([zero, st[:, :, half:]], axis=2)
    return jnp.concatenate([top, bot], axis=1)


def _pad_hist(cache):
    n, k, c = cache.shape
    return jnp.concatenate([jnp.zeros((n, HIST - k, c), cache.dtype), cache], axis=1)


def kernel(x_prompt, x_sample, c_prompt, c_sample, state_ssm, cache_conv_ssd, cache_attn_k,
           cache_attn_v, cache_conv_ffn, w_ada, b_ada, norm_mix, w_in, conv_ssd_w, conv_ssd_b,
           dt_bias, a_log, d_skip, ssd_norm, q_norm, k_norm, sinks, w_out, norm_ffn, w_up,
           conv_ffn_w, conv_ffn_b, w_down):
    depth = w_ada.shape[0]
    bp, seq, _ = x_prompt.shape
    bs, lsamp, _ = x_sample.shape
    rows_s = bs * lsamp
    tile = TILE_T
    n_blk = SAMPLE_SEQS

    mod_all = _ada_call(jnp.concatenate([c_prompt, c_sample], axis=0), w_ada, b_ada)
    mod_all = mod_all.reshape(depth, bp + bs, 6, D_MODEL)
    mod_p = jnp.pad(mod_all[:, :bp], ((0, 0), (0, 0), (0, 2), (0, 0)))
    mod_s = jnp.transpose(mod_all[:, bp:], (0, 2, 1, 3))

    consts = {
        "tril_p": _block_tril(tile, CHUNK),
        "tril_s": _block_tril(n_blk * lsamp, lsamp),
        "ones_q": _block_ones(MXU_TILE, HEAD_DIM),
        "ones_k": _block_ones(D_KV, HEAD_DIM),
        "smask": _state_mask(),
        "bias_p": _alibi_bias(CHUNK),
        "bias_s": _alibi_bias(lsamp),
    }

    xp = x_prompt
    xs = x_sample.reshape(rows_s, D_MODEL)
    outs = [[] for _ in range(10)]
    for l in range(depth):
        lw = _prep_layer(l, w_in, conv_ssd_w, conv_ssd_b, dt_bias, a_log, d_skip, ssd_norm, q_norm,
                         k_norm, w_out, norm_mix, norm_ffn, w_up, conv_ffn_w, conv_ffn_b, w_down)
        consts["sink_p"] = _sink_rows(sinks[l], CHUNK)
        consts["sink_s"] = _sink_rows(sinks[l], lsamp)

        xp, ssm_p, ct_p, k_p, v_p, ft_p = _layer_prompt_call(xp, mod_p[l], lw, consts, tile)

        xs, ssm_s, ct_s, k_s, v_s = _mixer_sample_call(
            xs, mod_s[l], _state_to_s2(state_ssm[l]),
            _pad_hist(cache_conv_ssd[l]), cache_attn_k[l].reshape(bs, WINDOW, D_KV),
            cache_attn_v[l].reshape(bs, WINDOW, D_KV), lw, consts, n_blk, lsamp)
        xs, ft_s = _ffn_sample_call(xs, mod_s[l], _pad_hist(cache_conv_ffn[l]), lw, n_blk, lsamp)

        kv_shape = (WINDOW, D_KV // HEAD_DIM, HEAD_DIM)
        for lst, val in zip(outs, (
                ssm_p.reshape(bp, SSD_HEADS, SSD_HEAD_DIM, D_STATE),
                ssm_s.reshape(bs, SSD_HEADS, SSD_HEAD_DIM, D_STATE),
                ct_p[:, HIST - (SSD_CONV - 1):], ct_s[:, HIST - (SSD_CONV - 1):],
                k_p.reshape((bp,) + kv_shape), k_s.reshape((bs,) + kv_shape),
                v_p.reshape((bp,) + kv_shape), v_s.reshape((bs,) + kv_shape),
                ft_p[:, HIST - (FFN_CONV - 1):], ft_s[:, HIST - (FFN_CONV - 1):])):
            lst.append(val)

    return (xp, xs.reshape(bs, lsamp, D_MODEL)) + tuple(jnp.stack(o) for o in outs)
```

```python
import functools

import numpy as np
import jax
import jax.numpy as jnp
from jax import lax
from jax.experimental import pallas as pl
from jax.experimental.pallas import tpu as pltpu

F32 = jnp.float32
BF16 = jnp.bfloat16

D_MODEL = 1024
SSD_HEADS = 8
SSD_HEAD_DIM = 64
D_SSD = 512
D_STATE = 64
D_BC = 128
SSD_CONV = 4
D_XBC = 768
ATTN_HEADS = 8
HEAD_DIM = 64
D_ATTN = 512
D_KV = 128
WINDOW = 128
CHUNK = 64
D_FF = 2816
FFN_CONV = 3
EPS = 1e-6

C_Z, C_XBC, C_DT, C_Q, C_K, C_V, C_END = 0, 512, 1280, 1408, 1920, 2048, 2176

MXU_TILE = 256
HIST = 8
FF_BLK = 256
N_FF_BLK = D_FF // FF_BLK
STEP_ORDER = "f" * 12 + "m" * 4 + "mf" * 8
TILE_T = 256
SAMPLE_SEQS = 16
VMEM_LIMIT = 56 * 1024 * 1024


def _dot(a, b):
    return jnp.dot(a, b, preferred_element_type=F32)


def _dot_nt(a, b):
    return lax.dot_general(a, b, (((1,), (1,)), ((), ())), preferred_element_type=F32)


def _dot_tn(a, b):
    return lax.dot_general(a, b, (((0,), (0,)), ((), ())), preferred_element_type=F32)


def _split_bf16(x, parts):
    out = []
    r = x
    for i in range(parts):
        p = r.astype(BF16)
        out.append(p)
        if i + 1 < parts:
            r = r - p.astype(F32)
    return out


def _dot_exact_rhs(a_bf16, x, parts):
    acc = None
    for p in _split_bf16(x, parts):
        d = _dot(a_bf16, p)
        acc = d if acc is None else acc + d
    return acc


def _dot_exact_lhs(x, b_bf16, parts):
    acc = None
    for p in _split_bf16(x, parts):
        d = _dot(p, b_bf16)
        acc = d if acc is None else acc + d
    return acc


def _rms(x, g):
    ms = jnp.mean(x * x, axis=-1, keepdims=True)
    return x * lax.rsqrt(ms + EPS) * g


def _silu(x):
    return x * (1.0 / (1.0 + jnp.exp(-x)))


def _softplus(x):
    return jnp.maximum(x, 0.0) + jnp.log1p(jnp.exp(-jnp.abs(x)))


def _lane_lo(shape):
    return (lax.broadcasted_iota(jnp.int32, shape, len(shape) - 1) % 128) < 64


def _causal_conv(buf_ref, raw3, w_ref, b_ref, taps):
    length = raw3.shape[1]
    buf_ref[:, HIST:HIST + length, :] = raw3
    y = b_ref[...]
    for k in range(taps):
        lo = HIST - (taps - 1) + k
        y = y + buf_ref[:, lo:lo + length, :] * w_ref[k:k + 1, :]
    return y


def _causal_conv_rolled(raw, hist, w_ref, b_ref, taps):
    row = lax.broadcasted_iota(jnp.int32, hist.shape, 0)
    y = b_ref[...]
    for k in range(taps):
        s = taps - 1 - k
        if s == 0:
            term = raw
        else:
            rolled = pltpu.roll(raw, s, axis=0)
            head = jnp.where(row < s, pltpu.roll(hist, s, axis=0), rolled[0:HIST])
            term = jnp.concatenate([head, rolled[HIST:]], axis=0)
        y = y + term * w_ref[k:k + 1, :]
    return y


def _ssd_steps(res, key, acs, xs, dt, bm, cm, state, dskip, smask, q):
    pad = CHUNK - q

    def stack2(a, b):
        if pad:
            zero = jnp.zeros((pad, a.shape[1]), a.dtype)
            return jnp.concatenate([a, zero, b, zero], axis=0)
        return jnp.concatenate([a, b], axis=0)

    lane = lax.broadcasted_iota(jnp.int32, (q, D_SSD), 1)
    row = lax.broadcasted_iota(jnp.int32, (q, D_SSD), 0)
    rowf = jnp.sum(jnp.where((lane % CHUNK) == row, acs, 0.0), axis=0, keepdims=True)
    a_end = acs[q - 1:q, :]
    xdt = xs * dt
    xw = (xdt * jnp.exp(a_end - acs)).astype(BF16)
    cm_b = cm.astype(BF16)
    zrow = jnp.zeros((2 * D_STATE - q, D_BC), F32)
    bm_t = jnp.concatenate([bm, zrow], axis=0).T.astype(BF16)
    xw_pad = jnp.concatenate([xw, jnp.zeros((2 * D_STATE - q, D_SSD), BF16)], axis=0)
    lo128 = _lane_lo((q, 128))
    bstack = stack2(bm, bm).astype(BF16)
    yield
    s_add = _dot(bm_t, xw_pad) * smask
    cbs = [_dot_nt(jnp.where(lo128 if g == 0 else ~lo128, cm, 0.0).astype(BF16), bstack)
           for g in range(2)]
    yield
    lane128 = lax.broadcasted_iota(jnp.int32, (q, 128), 1)
    row128 = lax.broadcasted_iota(jnp.int32, (q, 128), 0)
    causal = (lane128 % CHUNK) <= row128
    ms, rhss = [], []
    for j in range(4):
        seg = acs[:, 128 * j:128 * (j + 1)] - rowf[:, 128 * j:128 * (j + 1)]
        decay = jnp.exp(jnp.where(causal, seg, -jnp.inf))
        ms.append((cbs[j // 2] * decay).astype(BF16))
        xp = xdt[:, 128 * j:128 * (j + 1)]
        rhss.append(stack2(jnp.where(lo128, xp, 0.0), jnp.where(lo128, 0.0, xp)).astype(BF16))
    yield
    s2 = state["s2"]
    y_off = _dot(cm_b, s2.astype(BF16)) * jnp.exp(acs)
    state["s2"] = s2 * jnp.exp(a_end) + s_add
    y_diag = jnp.concatenate([_dot(m, r) for m, r in zip(ms, rhss)], axis=1)
    res[key] = y_diag + y_off + dskip * xs


def _dup_halves(x):
    lo = _lane_lo(x.shape)
    sw = pltpu.roll(x, 64, axis=1)
    return jnp.where(lo, x, sw).astype(BF16), jnp.where(lo, sw, x).astype(BF16)


def _attn_steps(res, key, qg, kd, vd, bias, sink, first_valid, q):
    band = kd.shape[0]
    lo_q = _lane_lo((q, 128))
    p0, p1 = qg[:, 0:128], qg[:, 128:256]
    qs = jnp.concatenate([jnp.where(lo_q, p0, 0.0), jnp.where(lo_q, 0.0, p0),
                          jnp.where(lo_q, p1, 0.0), jnp.where(lo_q, 0.0, p1)], axis=0)
    s = _dot_nt(kd, qs.astype(BF16))
    yield
    s = s * (HEAD_DIM ** -0.5) - bias
    if first_valid is not None:
        s = jnp.where(lax.broadcasted_iota(jnp.int32, (band, 1), 0) >= first_valid, s, -jnp.inf)
    m = jnp.maximum(jnp.max(s, axis=0, keepdims=True), sink)
    p = jnp.exp(s - m)
    p = (p / (jnp.sum(p, axis=0, keepdims=True) + jnp.exp(sink - m))).astype(BF16)
    yield
    og = _dot_tn(p, vd)
    yield
    res[key] = jnp.concatenate([jnp.where(lo_q, og[0:q], og[q:2 * q]),
                                jnp.where(lo_q, og[2 * q:3 * q], og[3 * q:4 * q])], axis=1)


def _run_staggered(gens):
    live, started = [], 0
    while started < len(gens) or live:
        if started < len(gens):
            live.append(gens[started])
            started += 1
        for g in list(live):
            try:
                next(g)
            except StopIteration:
                live.remove(g)
        yield


def _drain(gen):
    for _ in gen:
        pass


def _state_rows(s2):
    s_t = s2.T
    half = D_SSD // 2
    return jnp.concatenate([s_t[0:half, 0:D_STATE], s_t[half:D_SSD, D_STATE:2 * D_STATE]], axis=0)


def _head_rms(x, ones_ref, g_ref):
    w = ones_ref.shape[0]
    ones = ones_ref[...]
    sq = x * x
    ssq = jnp.concatenate([_dot_exact_lhs(sq[:, i:i + w], ones, 2)
                           for i in range(0, x.shape[1], w)], axis=1)
    return x * lax.rsqrt(ssq * (1.0 / HEAD_DIM) + EPS) * g_ref[...]


def _in_proj(h, win_ref, lo, hi):
    return _dot(h, win_ref[:, lo:hi])


def _expand_head_lanes(v):
    n = v.shape[0]
    lo = _lane_lo((n, 128))
    pairs = []
    for j in range(SSD_HEADS // 2):
        even = jnp.broadcast_to(v[:, 2 * j:2 * j + 1], (n, 128))
        odd = jnp.broadcast_to(v[:, 2 * j + 1:2 * j + 2], (n, 128))
        pairs.append(jnp.where(lo, even, odd))
    return jnp.concatenate(pairs, axis=1)


def _decay_terms(h, win_ref, dtb_ref, alog_ref, tril_ref):
    dt = _softplus(_in_proj(h, win_ref, C_DT, C_Q) + dtb_ref[...])
    acs = _dot_exact_rhs(tril_ref[...], dt * (-jnp.exp(alog_ref[...])), 3)
    return _expand_head_lanes(dt), _expand_head_lanes(acs)


def _run_ordered(order, gens):
    for who in order:
        next(gens[who], None)
    for g in gens.values():
        _drain(g)


def _ada_kernel(c_ref, w_ref, b_ref, o_ref):
    c = c_ref[...]
    o_ref[0] = _dot(_silu(c).astype(BF16), w_ref[0].astype(BF16)) + b_ref[0]


def _ada_call(c_all, w_ada, b_ada):
    depth = w_ada.shape[0]
    n = c_all.shape[0]
    blk = 1024
    return pl.pallas_call(
        _ada_kernel,
        grid=(depth, 6 * D_MODEL // blk),
        in_specs=[
            pl.BlockSpec((n, D_MODEL), lambda l, j: (0, 0)),
            pl.BlockSpec((1, D_MODEL, blk), lambda l, j: (l, 0, j)),
            pl.BlockSpec((1, 1, blk), lambda l, j: (l, 0, j)),
        ],
        out_specs=pl.BlockSpec((1, n, blk), lambda l, j: (l, 0, j)),
        out_shape=jax.ShapeDtypeStruct((depth, n, 6 * D_MODEL), F32),
        compiler_params=pltpu.CompilerParams(
            dimension_semantics=("arbitrary", "arbitrary"), vmem_limit_bytes=VMEM_LIMIT),
        name="adaln",
    )(c_all, w_ada, b_ada.reshape(depth, 1, 6 * D_MODEL))


def _ffn_norm(x, sh, sc, nffn_ref):
    return (_rms(x, nffn_ref[...]) * (1.0 + sc) + sh).astype(BF16)


def _ffn_steps(out_ref, x, h, g, hist_ref, wup_ref, cw_ref, cb_ref, wdn_ref, tail_ref,
               ubuf, gbuf, n_seq, length):
    rows = n_seq * length

    def up(j):
        return [_dot(h, wup_ref[:, base + j * FF_BLK:base + (j + 1) * FF_BLK]) for base in (0, D_FF)]

    acts = []
    raws = up(0)
    yield
    for j in range(N_FF_BLK):
        nxt = up(j + 1) if j + 1 < N_FF_BLK else None
        halves = []
        for raw, buf, base in zip(raws, (ubuf, gbuf), (0, D_FF)):
            lo = base + j * FF_BLK
            cw, cb = cw_ref.at[:, lo:lo + FF_BLK], cb_ref.at[:, lo:lo + FF_BLK]
            if n_seq == 1:
                y = _causal_conv_rolled(raw, hist_ref[0, :, lo:lo + FF_BLK], cw, cb, FFN_CONV)
                tail_ref[0, :, lo:lo + FF_BLK] = raw[length - HIST:length]
            else:
                buf[:, 0:HIST, :] = hist_ref[:, :, lo:lo + FF_BLK]
                y = _causal_conv(buf, raw.reshape(n_seq, length, FF_BLK), cw, cb, FFN_CONV)
                tail_ref[:, :, lo:lo + FF_BLK] = buf[:, length:length + HIST, :]
            halves.append(y.reshape(rows, FF_BLK))
        acts.append((_silu(halves[1]) * halves[0]).astype(BF16))
        raws = nxt
        yield
    k_split = (N_FF_BLK // 2 + 1) * FF_BLK
    act_a = jnp.concatenate(acts[:N_FF_BLK // 2 + 1], axis=1)
    act_b = jnp.concatenate(acts[N_FF_BLK // 2 + 1:], axis=1)
    for n in range(D_MODEL // FF_BLK):
        cols = slice(n * FF_BLK, (n + 1) * FF_BLK)
        d = _dot(act_a, wdn_ref[0:k_split, cols])
        yield
        d = d + _dot(act_b, wdn_ref[k_split:D_FF, cols])
        out_ref[:, cols] = x[:, cols] + g[:, cols] * d
        yield


def _mixer_prompt_steps(res, x, mod, t, nmix_ref, win_ref, cw_ref, cb_ref, dtb_ref, alog_ref,
                        dsk_ref, ssdn_ref, qn_ref, kn_ref, sink_ref, bias_ref, tril_ref, onesq_ref,
                        onesk_ref, smask_ref, wout_ref, xbc_buf, kbuf, vbuf, s2_ref, tile):
    n_chunks = tile // CHUNK
    sh1, sc1, g1 = mod[0:1], mod[1:2], mod[2:3]
    h = (_rms(x, nmix_ref[...]) * (1.0 + sc1) + sh1).astype(BF16)

    xbc_raw = _in_proj(h, win_ref, C_XBC, C_DT)
    yield
    conv = _causal_conv_rolled(xbc_raw, xbc_buf[...], cw_ref, cb_ref, SSD_CONV)
    res["tail"] = xbc_raw[tile - HIST:tile]
    xbc_buf[...] = res["tail"]
    xbc = _silu(conv)
    xs, bm, cm = xbc[:, 0:D_SSD], xbc[:, D_SSD:D_SSD + D_BC], xbc[:, D_SSD + D_BC:D_XBC]
    yield
    dt, acs = _decay_terms(h, win_ref, dtb_ref, alog_ref, tril_ref)
    yield
    qn = _head_rms(_in_proj(h, win_ref, C_Q, C_K), onesq_ref, qn_ref)
    kn = _head_rms(_in_proj(h, win_ref, C_K, C_V), onesk_ref, kn_ref)
    vn = _in_proj(h, win_ref, C_V, C_END)
    z = _in_proj(h, win_ref, C_Z, C_XBC)
    yield
    smask = smask_ref[...]
    dskip = dsk_ref[...]
    band = WINDOW + CHUNK
    kds = _dup_halves(jnp.concatenate([kbuf[...], kn], axis=0))
    vds = _dup_halves(jnp.concatenate([vbuf[...], vn], axis=0))
    res["k_last"] = kn[tile - WINDOW:tile]
    res["v_last"] = vn[tile - WINDOW:tile]
    kbuf[...] = res["k_last"]
    vbuf[...] = res["v_last"]
    state = {"s2": s2_ref[...]}
    part = {}
    items = []
    for c in range(n_chunks):
        r = slice(c * CHUNK, (c + 1) * CHUNK)
        kr = slice(c * CHUNK, c * CHUNK + band)
        first_valid = (WINDOW // CHUNK - (t * n_chunks + c)) * CHUNK
        items.append(_ssd_steps(part, ("y", c), acs[r], xs[r], dt[r], bm[r], cm[r], state, dskip,
                                smask, CHUNK))
        for g in range(2):
            items.append(_attn_steps(part, ("o", c, g), qn[r, 256 * g:256 * (g + 1)], kds[g][kr],
                                     vds[g][kr], bias_ref[g], sink_ref[g], first_valid, CHUNK))
    yield from _run_staggered(items)
    s2_ref[...] = state["s2"]
    y_ssd = _rms(jnp.concatenate([part["y", c] for c in range(n_chunks)], axis=0) * _silu(z),
                 ssdn_ref[...])
    o = jnp.concatenate([jnp.concatenate([part["o", c, 0], part["o", c, 1]], axis=1)
                         for c in range(n_chunks)], axis=0)
    yield
    mix = (_dot(y_ssd.astype(BF16), wout_ref[0:D_SSD, :])
           + _dot(o.astype(BF16), wout_ref[D_SSD:D_SSD + D_ATTN, :]))
    res["x_mid"] = x + g1 * mix


def _layer_prompt_kernel(x_ref, modm_ref, modf_ref,
                         nmix_ref, win_ref, cw_ref, cb_ref, dtb_ref, alog_ref, dsk_ref, ssdn_ref,
                         qn_ref, kn_ref, sink_ref, bias_ref, tril_ref, onesq_ref, onesk_ref,
                         smask_ref, wout_ref,
                         nffn_ref, wup_ref, fcw_ref, fcb_ref, wdn_ref,
                         xo_ref, ssm_ref, ctail_ref, ko_ref, vo_ref, ftail_ref,
                         xbc_buf, kbuf, vbuf, s2_ref, xmid_buf, hffn_buf, hist, *, tile, n_tiles):
    s = pl.program_id(0)
    t_mix = lax.rem(s, n_tiles)
    t_ffn = lax.rem(s + n_tiles - 1, n_tiles)
    slot_w = lax.rem(s, 2)
    slot_r = 1 - slot_w

    @pl.when(s == 0)
    def _():
        xmid_buf[...] = jnp.zeros(xmid_buf.shape, F32)
        hffn_buf[...] = jnp.zeros(hffn_buf.shape, BF16)

    @pl.when(t_mix == 0)
    def _():
        xbc_buf[...] = jnp.zeros((HIST, D_XBC), F32)
        kbuf[...] = jnp.zeros((WINDOW, D_KV), F32)
        vbuf[...] = jnp.zeros((WINDOW, D_KV), F32)
        s2_ref[...] = jnp.zeros((2 * D_STATE, D_SSD), F32)

    @pl.when(jnp.logical_or(t_ffn == 0, s == 0))
    def _():
        hist[...] = jnp.zeros(hist.shape, F32)

    mres = {}
    modm = modm_ref[0]
    mixer = _mixer_prompt_steps(
        mres, x_ref[0], modm, t_mix, nmix_ref, win_ref, cw_ref, cb_ref, dtb_ref, alog_ref,
        dsk_ref, ssdn_ref, qn_ref, kn_ref, sink_ref, bias_ref, tril_ref, onesq_ref, onesk_ref,
        smask_ref, wout_ref, xbc_buf, kbuf, vbuf, s2_ref, tile)
    ffn = _ffn_steps(xo_ref.at[0], xmid_buf[slot_r], hffn_buf[slot_r], modf_ref[0, 5:6, :], hist,
                     wup_ref, fcw_ref, fcb_ref, wdn_ref, hist, None, None, 1, tile)
    def mixer_then_norm():
        yield from mixer
        yield
        xmid_buf[slot_w] = mres["x_mid"]
        hffn_buf[slot_w] = _ffn_norm(mres["x_mid"], modm[3:4], modm[4:5], nffn_ref)

    _run_ordered(STEP_ORDER, {"f": ffn, "m": mixer_then_norm()})

    @pl.when(t_mix == n_tiles - 1)
    def _():
        ssm_ref[0] = _state_rows(s2_ref[...])
        ctail_ref[0] = mres["tail"]
        ko_ref[0] = mres["k_last"]
        vo_ref[0] = mres["v_last"]

    @pl.when(jnp.logical_and(t_ffn == n_tiles - 1, s > 0))
    def _():
        ftail_ref[...] = hist[...]


def _const_spec(shape):
    zeros = (0,) * len(shape)
    return pl.BlockSpec(shape, lambda *_: zeros, pipeline_mode=pl.Buffered(1))


def _layer_spec(a, l):
    zeros = (0,) * (a.ndim - 1)
    return pl.BlockSpec((None,) + a.shape[1:], lambda *_: (l,) + zeros,
                        pipeline_mode=pl.Buffered(1))


def _layer_prompt_call(x, mod, lw, consts, l, tile):
    b, length, _ = x.shape
    n_tiles = length // tile
    n_steps = b * n_tiles
    band = WINDOW + CHUNK
    kern = functools.partial(_layer_prompt_kernel, tile=tile, n_tiles=n_tiles)

    def mix_tile(s):
        return jnp.minimum(s, n_steps - 1)

    def ffn_tile(s):
        return jnp.maximum(s - 1, 0)

    lay = lambda a: (a, _layer_spec(a, l))
    con = lambda a: (a, _const_spec(a.shape))
    params = [lay(lw["norm_mix"]), lay(lw["w_in"]), lay(lw["conv_w"]), lay(lw["conv_b"]),
              lay(lw["dt_bias"]), lay(lw["a_log"]), lay(lw["d_skip"]), lay(lw["ssd_norm"]),
              lay(lw["q_norm"]), lay(lw["k_norm"]), lay(lw["sink_p"]), con(consts["bias_p"]),
              con(consts["tril_p"]), con(consts["ones_q"]), con(consts["ones_k"]),
              con(consts["smask"]), lay(lw["w_out"]),
              lay(lw["norm_ffn"]), lay(lw["w_up"]), lay(lw["conv_ffn_w"]), lay(lw["conv_ffn_b"]),
              lay(lw["w_down"])]
    assert consts["bias_p"].shape == (2, band, 4 * CHUNK)
    return pl.pallas_call(
        kern,
        grid=(n_steps + 1,),
        in_specs=[
            pl.BlockSpec((1, tile, D_MODEL),
                         lambda s: (mix_tile(s) // n_tiles, mix_tile(s) % n_tiles, 0)),
            pl.BlockSpec((None, 1, 8, D_MODEL), lambda s: (l, mix_tile(s) // n_tiles, 0, 0)),
            pl.BlockSpec((None, 1, 8, D_MODEL), lambda s: (l, ffn_tile(s) // n_tiles, 0, 0)),
        ] + [spec for _, spec in params],
        out_specs=[
            pl.BlockSpec((1, tile, D_MODEL),
                         lambda s: (ffn_tile(s) // n_tiles, ffn_tile(s) % n_tiles, 0)),
            pl.BlockSpec((1, D_SSD, D_STATE), lambda s: (mix_tile(s) // n_tiles, 0, 0)),
            pl.BlockSpec((1, HIST, D_XBC), lambda s: (mix_tile(s) // n_tiles, 0, 0)),
            pl.BlockSpec((1, WINDOW, D_KV), lambda s: (mix_tile(s) // n_tiles, 0, 0)),
            pl.BlockSpec((1, WINDOW, D_KV), lambda s: (mix_tile(s) // n_tiles, 0, 0)),
            pl.BlockSpec((1, HIST, 2 * D_FF), lambda s: (ffn_tile(s) // n_tiles, 0, 0)),
        ],
        out_shape=[
            jax.ShapeDtypeStruct((b, length, D_MODEL), F32),
            jax.ShapeDtypeStruct((b, D_SSD, D_STATE), F32),
            jax.ShapeDtypeStruct((b, HIST, D_XBC), F32),
            jax.ShapeDtypeStruct((b, WINDOW, D_KV), F32),
            jax.ShapeDtypeStruct((b, WINDOW, D_KV), F32),
            jax.ShapeDtypeStruct((b, HIST, 2 * D_FF), F32),
        ],
        scratch_shapes=[
            pltpu.VMEM((HIST, D_XBC), F32),
            pltpu.VMEM((WINDOW, D_KV), F32),
            pltpu.VMEM((WINDOW, D_KV), F32),
            pltpu.VMEM((2 * D_STATE, D_SSD), F32),
            pltpu.VMEM((2, tile, D_MODEL), F32),
            pltpu.VMEM((2, tile, D_MODEL), BF16),
            pltpu.VMEM((1, HIST, 2 * D_FF), F32),
        ],
        compiler_params=pltpu.CompilerParams(
            dimension_semantics=("arbitrary",), vmem_limit_bytes=VMEM_LIMIT),
        name="layer_prompt",
    )(x, mod, mod, *[a for a, _ in params])


def _seq_rows(mod_ref, k, length):
    v = mod_ref[k]
    n_seq, d = v.shape
    return jnp.broadcast_to(v[:, None, :], (n_seq, length, d)).reshape(n_seq * length, d)


def _mixer_sample_kernel(x_ref, mod_ref, ssm_in_ref, ccache_ref, kc_ref, vc_ref,
                         nmix_ref, win_ref, cw_ref, cb_ref, dtb_ref, alog_ref,
                         dsk_ref, ssdn_ref, qn_ref, kn_ref, sink_ref, bias_ref, tril_ref,
                         onesq_ref, onesk_ref, smask_ref, wout_ref,
                         xo_ref, ssm_ref, ctail_ref, ko_ref, vo_ref,
                         xbc_buf, xs_buf, bc_buf, dt_buf, acs_buf, q_buf, k_buf, v_buf, y_buf, o_buf,
                         *, n_seq, length):
    rows = n_seq * length
    x = x_ref[...]
    h = (_rms(x, nmix_ref[...]) * (1.0 + _seq_rows(mod_ref, 1, length))
         + _seq_rows(mod_ref, 0, length)).astype(BF16)

    xbc_raw = _in_proj(h, win_ref, C_XBC, C_DT)
    xbc_buf[:, 0:HIST, :] = ccache_ref[...]
    conv = _causal_conv(xbc_buf, xbc_raw.reshape(n_seq, length, D_XBC), cw_ref, cb_ref, SSD_CONV)
    ctail_ref[...] = xbc_buf[:, length:length + HIST, :]
    xbc = _silu(conv.reshape(rows, D_XBC))
    xs_buf[...] = xbc[:, 0:D_SSD]
    bc_buf[...] = xbc[:, D_SSD:D_XBC]
    dt_buf[...], acs_buf[...] = _decay_terms(h, win_ref, dtb_ref, alog_ref, tril_ref)
    q_buf[...] = _head_rms(_in_proj(h, win_ref, C_Q, C_K), onesq_ref, qn_ref)
    k_buf[...] = _head_rms(_in_proj(h, win_ref, C_K, C_V), onesk_ref, kn_ref)
    v_buf[...] = _in_proj(h, win_ref, C_V, C_END)

    def body(i, carry):
        r = pl.ds(pl.multiple_of(i * length, length), length)
        bc = bc_buf[r, :]
        part = {}
        state = {"s2": ssm_in_ref[i]}
        _drain(_ssd_steps(part, "y", acs_buf[r, :], xs_buf[r, :], dt_buf[r, :], bc[:, 0:D_BC],
                          bc[:, D_BC:2 * D_BC], state, dsk_ref[...], smask_ref[...], length))
        ssm_ref[i] = _state_rows(state["s2"])
        y_buf[r, :] = part["y"]
        kband = jnp.concatenate([kc_ref[i], k_buf[r, :]], axis=0)
        vband = jnp.concatenate([vc_ref[i], v_buf[r, :]], axis=0)
        ko_ref[i] = kband[length:length + WINDOW]
        vo_ref[i] = vband[length:length + WINDOW]
        kds, vds = _dup_halves(kband), _dup_halves(vband)
        q_i = q_buf[r, :]
        for g in range(2):
            _drain(_attn_steps(part, ("o", g), q_i[:, 256 * g:256 * (g + 1)], kds[g], vds[g],
                               bias_ref[g], sink_ref[g], None, length))
        o_buf[r, :] = jnp.concatenate([part["o", 0], part["o", 1]], axis=1)
        return carry

    lax.fori_loop(0, n_seq, body, 0)

    z = _in_proj(h, win_ref, C_Z, C_XBC)
    y_ssd = _rms(y_buf[...] * _silu(z), ssdn_ref[...])
    mix = (_dot(y_ssd.astype(BF16), wout_ref[0:D_SSD, :])
           + _dot(o_buf[...].astype(BF16), wout_ref[D_SSD:D_SSD + D_ATTN, :]))
    xo_ref[...] = x + _seq_rows(mod_ref, 2, length) * mix


def _lead_spec(shape):
    tail_zeros = (0,) * (len(shape) - 1)
    return pl.BlockSpec(shape, lambda i: (i,) + tail_zeros)


def _mod_spec(n_blk, l):
    return pl.BlockSpec((None, 6, n_blk, D_MODEL), lambda i: (l, 0, i, 0))


def _seq_spec(a, n_blk, l):
    zeros = (0,) * (a.ndim - 2)
    return pl.BlockSpec((None, n_blk) + a.shape[2:], lambda i: (l, i) + zeros)


def _mixer_sample_call(x2, mod, ssm_in, ccache, kc, vc, lw, consts, l, n_blk, length):
    n_seq = ssm_in.shape[1]
    rows = n_blk * length
    kern = functools.partial(_mixer_sample_kernel, n_seq=n_blk, length=length)
    seq_args = (ssm_in, ccache, kc, vc)
    lay = lambda a: (a, _layer_spec(a, l))
    con = lambda a: (a, _const_spec(a.shape))
    params = [lay(lw["norm_mix"]), lay(lw["w_in"]), lay(lw["conv_w"]), lay(lw["conv_b"]),
              lay(lw["dt_bias"]), lay(lw["a_log"]), lay(lw["d_skip"]), lay(lw["ssd_norm"]),
              lay(lw["q_norm"]), lay(lw["k_norm"]), lay(lw["sink_s"]), con(consts["bias_s"]),
              con(consts["tril_s"]), con(consts["ones_q"]), con(consts["ones_k"]),
              con(consts["smask"]), lay(lw["w_out"])]
    out_shape = [
        jax.ShapeDtypeStruct((n_seq * length, D_MODEL), F32),
        jax.ShapeDtypeStruct((n_seq, D_SSD, D_STATE), F32),
        jax.ShapeDtypeStruct((n_seq, HIST, D_XBC), F32),
        jax.ShapeDtypeStruct((n_seq, WINDOW, D_KV), F32),
        jax.ShapeDtypeStruct((n_seq, WINDOW, D_KV), F32),
    ]
    return pl.pallas_call(
        kern,
        grid=(n_seq // n_blk,),
        in_specs=([_lead_spec((rows, D_MODEL)), _mod_spec(n_blk, l)]
                  + [_seq_spec(a, n_blk, l) for a in seq_args]
                  + [spec for _, spec in params]),
        out_specs=([_lead_spec((rows, D_MODEL))]
                   + [_lead_spec((n_blk,) + s.shape[1:]) for s in out_shape[1:]]),
        out_shape=out_shape,
        scratch_shapes=[
            pltpu.VMEM((n_blk, HIST + length, D_XBC), F32),
            pltpu.VMEM((rows, D_SSD), F32),
            pltpu.VMEM((rows, 2 * D_BC), F32),
            pltpu.VMEM((rows, D_SSD), F32),
            pltpu.VMEM((rows, D_SSD), F32),
            pltpu.VMEM((rows, D_ATTN), F32),
            pltpu.VMEM((rows, D_KV), F32),
            pltpu.VMEM((rows, D_KV), F32),
            pltpu.VMEM((rows, D_SSD), F32),
            pltpu.VMEM((rows, D_ATTN), F32),
        ],
        compiler_params=pltpu.CompilerParams(
            dimension_semantics=("arbitrary",), vmem_limit_bytes=VMEM_LIMIT),
        name="mixer_sample",
    )(x2, mod, *seq_args, *[a for a, _ in params])


def _ffn_sample_kernel(x_ref, mod_ref, hist_ref, nffn_ref, wup_ref, cw_ref, cb_ref,
                       wdn_ref, xo_ref, tail_ref, ubuf, gbuf, *, n_seq, length):
    x = x_ref[...]
    h = _ffn_norm(x, _seq_rows(mod_ref, 3, length), _seq_rows(mod_ref, 4, length), nffn_ref)
    _drain(_ffn_steps(xo_ref, x, h, _seq_rows(mod_ref, 5, length), hist_ref, wup_ref, cw_ref,
                      cb_ref, wdn_ref, tail_ref, ubuf, gbuf, n_seq, length))


def _ffn_sample_call(x2, mod, hist, lw, l, n_blk, length):
    n_seq = hist.shape[1]
    rows = n_blk * length
    kern = functools.partial(_ffn_sample_kernel, n_seq=n_blk, length=length)
    const_args = (lw["norm_ffn"], lw["w_up"], lw["conv_ffn_w"], lw["conv_ffn_b"], lw["w_down"])
    out_shape = [
        jax.ShapeDtypeStruct((n_seq * length, D_MODEL), F32),
        jax.ShapeDtypeStruct((n_seq, HIST, 2 * D_FF), F32),
    ]
    return pl.pallas_call(
        kern,
        grid=(n_seq // n_blk,),
        in_specs=([_lead_spec((rows, D_MODEL)), _mod_spec(n_blk, l), _seq_spec(hist, n_blk, l)]
                  + [_layer_spec(a, l) for a in const_args]),
        out_specs=[_lead_spec((rows, D_MODEL)), _lead_spec((n_blk, HIST, 2 * D_FF))],
        out_shape=out_shape,
        scratch_shapes=[
            pltpu.VMEM((n_blk, HIST + length, FF_BLK), F32),
            pltpu.VMEM((n_blk, HIST + length, FF_BLK), F32),
        ],
        compiler_params=pltpu.CompilerParams(
            dimension_semantics=("arbitrary",), vmem_limit_bytes=VMEM_LIMIT),
        name="ffn_sample",
    )(x2, mod, hist, *const_args)


def _expand_heads(v):
    return jnp.repeat(v.astype(F32), SSD_HEAD_DIM, axis=1)[:, None, :]


def _pad_heads(v):
    return jnp.pad(v.astype(F32), ((0, 0), (0, C_Q - C_DT - SSD_HEADS)))[:, None, :]


def _prep_params(w_in, conv_ssd_w, conv_ssd_b, dt_bias, a_log, d_skip, ssd_norm, q_norm, k_norm,
                 sinks, w_out, norm_mix, norm_ffn, w_up, conv_ffn_w, conv_ffn_b, w_down, q_sample):
    o_dt = D_SSD + D_XBC
    o_q = o_dt + SSD_HEADS
    w_dt = jnp.pad(w_in[:, :, o_dt:o_q], ((0, 0), (0, 0), (0, C_Q - C_DT - SSD_HEADS)))
    w_in_r = jnp.concatenate(
        [w_in[:, :, :o_dt], w_dt, w_in[:, :, o_q:]], axis=2).astype(BF16)
    assert w_in_r.shape[2] == C_END
    return {
        "norm_mix": norm_mix[:, None, :],
        "w_in": w_in_r,
        "conv_w": conv_ssd_w,
        "conv_b": conv_ssd_b[:, None, :],
        "dt_bias": _pad_heads(dt_bias),
        "a_log": _pad_heads(a_log),
        "d_skip": _expand_heads(d_skip),
        "ssd_norm": ssd_norm[:, None, :],
        "q_norm": jnp.tile(q_norm, (1, ATTN_HEADS))[:, None, :],
        "k_norm": jnp.tile(k_norm, (1, D_KV // HEAD_DIM))[:, None, :],
        "sink_p": _sink_rows(sinks, CHUNK),
        "sink_s": _sink_rows(sinks, q_sample),
        "w_out": w_out.astype(BF16),
        "norm_ffn": norm_ffn[:, None, :],
        "w_up": w_up.astype(BF16),
        "conv_ffn_w": conv_ffn_w,
        "conv_ffn_b": conv_ffn_b[:, None, :],
        "w_down": w_down.astype(BF16),
    }


def _block_tril(n, blk):
    i = np.arange(n)
    return jnp.asarray(((i[:, None] // blk) == (i[None, :] // blk)) & (i[None, :] <= i[:, None]),
                       dtype=BF16)


def _block_ones(n, blk):
    i = np.arange(n)
    return jnp.asarray((i[:, None] // blk) == (i[None, :] // blk), dtype=BF16)


def _alibi_bias(q):
    slopes = (2.0 ** (-8.0 * np.arange(1, ATTN_HEADS + 1) / ATTN_HEADS)).astype(np.float32)
    dist = np.abs(np.arange(q)[:, None] + WINDOW - np.arange(WINDOW + q)[None, :]).astype(np.float32)
    bias = slopes[:, None, None] * dist[None]
    return jnp.asarray(np.transpose(bias.reshape(2, 4 * q, WINDOW + q), (0, 2, 1)))


def _sink_rows(sinks, q):
    return jnp.repeat(sinks.astype(F32), q, axis=1).reshape(sinks.shape[0], 2, 1, 4 * q)


def _state_mask():
    r = np.arange(2 * D_STATE)[:, None] // D_STATE
    c = np.arange(D_SSD)[None, :] // (D_SSD // 2)
    return jnp.asarray(r == c, dtype=F32)


def _state_to_s2(state):
    lead = state.shape[:2]
    st = jnp.transpose(state, (0, 1, 4, 2, 3)).reshape(lead + (D_STATE, D_SSD))
    half = D_SSD // 2
    zero = jnp.zeros(lead + (D_STATE, half), state.dtype)
    top = jnp.concatenate([st[..., :half], zero], axis=-1)
    bot = jnp.concatenate([zero, st[..., half:]], axis=-1)
    return jnp.concatenate([top, bot], axis=-2)


def _pad_hist(cache):
    k = cache.shape[2]
    return jnp.pad(cache, ((0, 0), (0, 0), (HIST - k, 0), (0, 0)))


def kernel(x_prompt, x_sample, c_prompt, c_sample, state_ssm, cache_conv_ssd, cache_attn_k,
           cache_attn_v, cache_conv_ffn, w_ada, b_ada, norm_mix, w_in, conv_ssd_w, conv_ssd_b,
           dt_bias, a_log, d_skip, ssd_norm, q_norm, k_norm, sinks, w_out, norm_ffn, w_up,
           conv_ffn_w, conv_ffn_b, w_down):
    depth = w_ada.shape[0]
    bp, seq, _ = x_prompt.shape
    bs, lsamp, _ = x_sample.shape
    rows_s = bs * lsamp
    tile = TILE_T
    n_blk = SAMPLE_SEQS

    mod_all = _ada_call(jnp.concatenate([c_prompt, c_sample], axis=0), w_ada, b_ada)
    mod_all = mod_all.reshape(depth, bp + bs, 6, D_MODEL)
    mod_p = jnp.pad(mod_all[:, :bp], ((0, 0), (0, 0), (0, 2), (0, 0)))
    mod_s = jnp.transpose(mod_all[:, bp:], (0, 2, 1, 3))

    consts = {
        "tril_p": _block_tril(tile, CHUNK),
        "tril_s": _block_tril(n_blk * lsamp, lsamp),
        "ones_q": _block_ones(MXU_TILE, HEAD_DIM),
        "ones_k": _block_ones(D_KV, HEAD_DIM),
        "smask": _state_mask(),
        "bias_p": _alibi_bias(CHUNK),
        "bias_s": _alibi_bias(lsamp),
    }

    lw = _prep_params(w_in, conv_ssd_w, conv_ssd_b, dt_bias, a_log, d_skip, ssd_norm, q_norm, k_norm,
                      sinks, w_out, norm_mix, norm_ffn, w_up, conv_ffn_w, conv_ffn_b, w_down, lsamp)
    ssm_in = _state_to_s2(state_ssm)
    conv_in = _pad_hist(cache_conv_ssd)
    ffn_in = _pad_hist(cache_conv_ffn)
    k_in = cache_attn_k.reshape(depth, bs, WINDOW, D_KV)
    v_in = cache_attn_v.reshape(depth, bs, WINDOW, D_KV)

    xp = x_prompt
    xs = x_sample.reshape(rows_s, D_MODEL)
    outs = [[] for _ in range(10)]
    for l in range(depth):
        xp, ssm_p, ct_p, k_p, v_p, ft_p = _layer_prompt_call(xp, mod_p, lw, consts, l, tile)
        xs, ssm_s, ct_s, k_s, v_s = _mixer_sample_call(
            xs, mod_s, ssm_in, conv_in, k_in, v_in, lw, consts, l, n_blk, lsamp)
        xs, ft_s = _ffn_sample_call(xs, mod_s, ffn_in, lw, l, n_blk, lsamp)

        kv_shape = (WINDOW, D_KV // HEAD_DIM, HEAD_DIM)
        for lst, val in zip(outs, (
                ssm_p.reshape(bp, SSD_HEADS, SSD_HEAD_DIM, D_STATE),
                ssm_s.reshape(bs, SSD_HEADS, SSD_HEAD_DIM, D_STATE),
                ct_p[:, HIST - (SSD_CONV - 1):], ct_s[:, HIST - (SSD_CONV - 1):],
                k_p.reshape((bp,) + kv_shape), k_s.reshape((bs,) + kv_shape),
                v_p.reshape((bp,) + kv_shape), v_s.reshape((bs,) + kv_shape),
                ft_p[:, HIST - (FFN_CONV - 1):], ft_s[:, HIST - (FFN_CONV - 1):])):
            lst.append(val)

    return (xp, xs.reshape(bs, lsamp, D_MODEL)) + tuple(jnp.stack(o) for o in outs)
```

```python
import functools

import numpy as np
import jax
import jax.numpy as jnp
from jax import lax
from jax.experimental import pallas as pl
from jax.experimental.pallas import tpu as pltpu

F32 = jnp.float32
BF16 = jnp.bfloat16

D_MODEL = 1024
SSD_HEADS = 8
SSD_HEAD_DIM = 64
D_SSD = 512
D_STATE = 64
D_BC = 128
SSD_CONV = 4
D_XBC = 768
ATTN_HEADS = 8
HEAD_DIM = 64
D_ATTN = 512
D_KV = 128
WINDOW = 128
CHUNK = 64
D_FF = 2816
FFN_CONV = 3
EPS = 1e-6

C_Z, C_XBC, C_DT, C_Q, C_K, C_V, C_END = 0, 512, 1280, 1408, 1920, 2048, 2176

MXU_TILE = 256
HIST = 8
FF_BLK = 256
N_FF_BLK = D_FF // FF_BLK
STEP_ORDER = "f" * 12 + "m" * 4 + "mf" * 8
TILE_T = 256
SAMPLE_SEQS = 16
VMEM_LIMIT = 56 * 1024 * 1024


def _dot(a, b):
    return jnp.dot(a, b, preferred_element_type=F32)


def _dot_nt(a, b):
    return lax.dot_general(a, b, (((1,), (1,)), ((), ())), preferred_element_type=F32)


def _dot_tn(a, b):
    return lax.dot_general(a, b, (((0,), (0,)), ((), ())), preferred_element_type=F32)


def _split_bf16(x, parts):
    out = []
    r = x
    for i in range(parts):
        p = r.astype(BF16)
        out.append(p)
        if i + 1 < parts:
            r = r - p.astype(F32)
    return out


def _dot_exact_rhs(a_bf16, x, parts):
    acc = None
    for p in _split_bf16(x, parts):
        d = _dot(a_bf16, p)
        acc = d if acc is None else acc + d
    return acc


def _dot_exact_lhs(x, b_bf16, parts):
    acc = None
    for p in _split_bf16(x, parts):
        d = _dot(p, b_bf16)
        acc = d if acc is None else acc + d
    return acc


def _rms(x, g):
    ms = jnp.mean(x * x, axis=-1, keepdims=True)
    return x * lax.rsqrt(ms + EPS) * g


def _silu(x):
    return x * (1.0 / (1.0 + jnp.exp2(x * (-np.log2(np.e)))))


def _softplus(x):
    return jnp.maximum(x, 0.0) + jnp.log1p(jnp.exp(-jnp.abs(x)))


def _lane_lo(shape):
    return (lax.broadcasted_iota(jnp.int32, shape, len(shape) - 1) % 128) < 64


def _causal_conv(buf_ref, raw3, w_ref, b_ref, taps):
    length = raw3.shape[1]
    buf_ref[:, HIST:HIST + length, :] = raw3
    y = b_ref[...]
    for k in range(taps):
        lo = HIST - (taps - 1) + k
        y = y + buf_ref[:, lo:lo + length, :] * w_ref[k:k + 1, :]
    return y


def _causal_conv_rolled(raw, hist, w_ref, b_ref, taps):
    row = lax.broadcasted_iota(jnp.int32, hist.shape, 0)
    y = b_ref[...]
    for k in range(taps):
        s = taps - 1 - k
        if s == 0:
            term = raw
        else:
            rolled = pltpu.roll(raw, s, axis=0)
            head = jnp.where(row < s, pltpu.roll(hist, s, axis=0), rolled[0:HIST])
            term = jnp.concatenate([head, rolled[HIST:]], axis=0)
        y = y + term * w_ref[k:k + 1, :]
    return y


def _ssd_steps(res, key, acs, xs, dt, bm, cm, state, dskip, smask, q):
    pad = CHUNK - q

    def stack2(a, b):
        if pad:
            zero = jnp.zeros((pad, a.shape[1]), a.dtype)
            return jnp.concatenate([a, zero, b, zero], axis=0)
        return jnp.concatenate([a, b], axis=0)

    lane = lax.broadcasted_iota(jnp.int32, (q, D_SSD), 1)
    row = lax.broadcasted_iota(jnp.int32, (q, D_SSD), 0)
    rowf = jnp.sum(jnp.where((lane % CHUNK) == row, acs, 0.0), axis=0, keepdims=True)
    a_end = acs[q - 1:q, :]
    xdt = xs * dt
    xw = (xdt * jnp.exp(a_end - acs)).astype(BF16)
    cm_b = cm.astype(BF16)
    zrow = jnp.zeros((2 * D_STATE - q, D_BC), F32)
    bm_t = jnp.concatenate([bm, zrow], axis=0).T.astype(BF16)
    xw_pad = jnp.concatenate([xw, jnp.zeros((2 * D_STATE - q, D_SSD), BF16)], axis=0)
    lo128 = _lane_lo((q, 128))
    bstack = stack2(bm, bm).astype(BF16)
    yield
    s_add = _dot(bm_t, xw_pad) * smask
    cbs = [_dot_nt(jnp.where(lo128 if g == 0 else ~lo128, cm, 0.0).astype(BF16), bstack)
           for g in range(2)]
    yield
    lane128 = lax.broadcasted_iota(jnp.int32, (q, 128), 1)
    row128 = lax.broadcasted_iota(jnp.int32, (q, 128), 0)
    causal = (lane128 % CHUNK) <= row128
    ms, rhss = [], []
    for j in range(4):
        seg = acs[:, 128 * j:128 * (j + 1)] - rowf[:, 128 * j:128 * (j + 1)]
        decay = jnp.exp(jnp.where(causal, seg, -jnp.inf))
        ms.append((cbs[j // 2] * decay).astype(BF16))
        xp = xdt[:, 128 * j:128 * (j + 1)]
        rhss.append(stack2(jnp.where(lo128, xp, 0.0), jnp.where(lo128, 0.0, xp)).astype(BF16))
    yield
    s2 = state["s2"]
    y_off = _dot(cm_b, s2.astype(BF16)) * jnp.exp(acs)
    state["s2"] = s2 * jnp.exp(a_end) + s_add
    y_diag = jnp.concatenate([_dot(m, r) for m, r in zip(ms, rhss)], axis=1)
    res[key] = y_diag + y_off + dskip * xs


def _dup_halves(x):
    lo = _lane_lo(x.shape)
    sw = pltpu.roll(x, 64, axis=1)
    return jnp.where(lo, x, sw).astype(BF16), jnp.where(lo, sw, x).astype(BF16)


def _attn_steps(res, key, qg, kd, vd, bias, sink, first_valid, q):
    band = kd.shape[0]
    lo_q = _lane_lo((q, 128))
    p0, p1 = qg[:, 0:128], qg[:, 128:256]
    qs = jnp.concatenate([jnp.where(lo_q, p0, 0.0), jnp.where(lo_q, 0.0, p0),
                          jnp.where(lo_q, p1, 0.0), jnp.where(lo_q, 0.0, p1)], axis=0)
    s = _dot_nt(kd, qs.astype(BF16))
    yield
    s = s * (HEAD_DIM ** -0.5) - bias
    if first_valid is not None:
        s = jnp.where(lax.broadcasted_iota(jnp.int32, (band, 1), 0) >= first_valid, s, -jnp.inf)
    m = jnp.maximum(jnp.max(s, axis=0, keepdims=True), sink)
    p = jnp.exp(s - m)
    p = (p / (jnp.sum(p, axis=0, keepdims=True) + jnp.exp(sink - m))).astype(BF16)
    yield
    og = _dot_tn(p, vd)
    yield
    res[key] = jnp.concatenate([jnp.where(lo_q, og[0:q], og[q:2 * q]),
                                jnp.where(lo_q, og[2 * q:3 * q], og[3 * q:4 * q])], axis=1)


def _run_staggered(gens, stagger=True):
    live, started = [], 0
    while started < len(gens) or live:
        while started < len(gens):
            live.append(gens[started])
            started += 1
            if stagger:
                break
        for g in list(live):
            try:
                next(g)
            except StopIteration:
                live.remove(g)
        yield


def _drain(gen):
    for _ in gen:
        pass


def _state_rows(s2):
    s_t = s2.T
    half = D_SSD // 2
    return jnp.concatenate([s_t[0:half, 0:D_STATE], s_t[half:D_SSD, D_STATE:2 * D_STATE]], axis=0)


def _head_rms(x, ones_ref, g_ref):
    w = ones_ref.shape[0]
    ones = ones_ref[...]
    sq = x * x
    ssq = jnp.concatenate([_dot_exact_lhs(sq[:, i:i + w], ones, 2)
                           for i in range(0, x.shape[1], w)], axis=1)
    return x * lax.rsqrt(ssq * (1.0 / HEAD_DIM) + EPS) * g_ref[...]


def _in_proj(h, win_ref, lo, hi):
    return _dot(h, win_ref[:, lo:hi])


def _expand_head_lanes(v):
    n = v.shape[0]
    lo = _lane_lo((n, 128))
    pairs = []
    for j in range(SSD_HEADS // 2):
        even = jnp.broadcast_to(v[:, 2 * j:2 * j + 1], (n, 128))
        odd = jnp.broadcast_to(v[:, 2 * j + 1:2 * j + 2], (n, 128))
        pairs.append(jnp.where(lo, even, odd))
    return jnp.concatenate(pairs, axis=1)


def _decay_terms(h, win_ref, dtb_ref, alog_ref, tril_ref):
    dt = _softplus(_in_proj(h, win_ref, C_DT, C_Q) + dtb_ref[...])
    acs = _dot_exact_rhs(tril_ref[...], dt * (-jnp.exp(alog_ref[...])), 3)
    return _expand_head_lanes(dt), _expand_head_lanes(acs)


def _run_ordered(order, gens):
    for who in order:
        next(gens[who], None)
    for g in gens.values():
        _drain(g)


def _ada_kernel(c_ref, w_ref, b_ref, o_ref):
    c = c_ref[...]
    o_ref[0] = _dot(_silu(c).astype(BF16), w_ref[0].astype(BF16)) + b_ref[0]


def _ada_call(c_all, w_ada, b_ada):
    depth = w_ada.shape[0]
    n = c_all.shape[0]
    blk = 1024
    return pl.pallas_call(
        _ada_kernel,
        grid=(depth, 6 * D_MODEL // blk),
        in_specs=[
            pl.BlockSpec((n, D_MODEL), lambda l, j: (0, 0)),
            pl.BlockSpec((1, D_MODEL, blk), lambda l, j: (l, 0, j)),
            pl.BlockSpec((1, 1, blk), lambda l, j: (l, 0, j)),
        ],
        out_specs=pl.BlockSpec((1, n, blk), lambda l, j: (l, 0, j)),
        out_shape=jax.ShapeDtypeStruct((depth, n, 6 * D_MODEL), F32),
        compiler_params=pltpu.CompilerParams(
            dimension_semantics=("arbitrary", "arbitrary"), vmem_limit_bytes=VMEM_LIMIT),
        name="adaln",
    )(c_all, w_ada, b_ada.reshape(depth, 1, 6 * D_MODEL))


def _ffn_norm(x, sh, sc, nffn_ref):
    return (_rms(x, nffn_ref[...]) * (1.0 + sc) + sh).astype(BF16)


def _ffn_steps(put_out, x, h, g, hist_ref, wup_ref, cw_ref, cb_ref, wdn_ref, tail_ref,
               ubuf, gbuf, n_seq, length):
    rows = n_seq * length

    def up(j):
        return [_dot(h, wup_ref[:, base + j * FF_BLK:base + (j + 1) * FF_BLK]) for base in (0, D_FF)]

    acts = []
    raws = up(0)
    yield
    for j in range(N_FF_BLK):
        nxt = up(j + 1) if j + 1 < N_FF_BLK else None
        halves = []
        for raw, buf, base in zip(raws, (ubuf, gbuf), (0, D_FF)):
            lo = base + j * FF_BLK
            cw, cb = cw_ref.at[:, lo:lo + FF_BLK], cb_ref.at[:, lo:lo + FF_BLK]
            if n_seq == 1:
                y = _causal_conv_rolled(raw, hist_ref[0, :, lo:lo + FF_BLK], cw, cb, FFN_CONV)
                tail_ref[0, :, lo:lo + FF_BLK] = raw[length - HIST:length]
            else:
                buf[:, 0:HIST, :] = hist_ref[:, :, lo:lo + FF_BLK]
                y = _causal_conv(buf, raw.reshape(n_seq, length, FF_BLK), cw, cb, FFN_CONV)
                tail_ref[:, :, lo:lo + FF_BLK] = buf[:, length:length + HIST, :]
            halves.append(y.reshape(rows, FF_BLK))
        acts.append((_silu(halves[1]) * halves[0]).astype(BF16))
        raws = nxt
        yield
    k_split = (N_FF_BLK // 2 + 1) * FF_BLK
    act_a = jnp.concatenate(acts[:N_FF_BLK // 2 + 1], axis=1)
    act_b = jnp.concatenate(acts[N_FF_BLK // 2 + 1:], axis=1)
    for n in range(D_MODEL // FF_BLK):
        cols = slice(n * FF_BLK, (n + 1) * FF_BLK)
        d = _dot(act_a, wdn_ref[0:k_split, cols])
        yield
        d = d + _dot(act_b, wdn_ref[k_split:D_FF, cols])
        put_out(cols, x[:, cols] + g[:, cols] * d)
        yield


def _mixer_prompt_steps(res, x, mod, t, nmix_ref, win_ref, cw_ref, cb_ref, dtb_ref, alog_ref,
                        dsk_ref, ssdn_ref, qn_ref, kn_ref, sink_ref, bias_ref, tril_ref, onesq_ref,
                        onesk_ref, smask_ref, wout_ref, xbc_buf, kbuf, vbuf, s2_ref, tile):
    n_chunks = tile // CHUNK
    sh1, sc1, g1 = mod[0:1], mod[1:2], mod[2:3]
    h = (_rms(x, nmix_ref[...]) * (1.0 + sc1) + sh1).astype(BF16)

    xbc_raw = _in_proj(h, win_ref, C_XBC, C_DT)
    yield
    conv = _causal_conv_rolled(xbc_raw, xbc_buf[...], cw_ref, cb_ref, SSD_CONV)
    res["tail"] = xbc_raw[tile - HIST:tile]
    xbc_buf[...] = res["tail"]
    xbc = _silu(conv)
    xs, bm, cm = xbc[:, 0:D_SSD], xbc[:, D_SSD:D_SSD + D_BC], xbc[:, D_SSD + D_BC:D_XBC]
    yield
    dt, acs = _decay_terms(h, win_ref, dtb_ref, alog_ref, tril_ref)
    yield
    qn = _head_rms(_in_proj(h, win_ref, C_Q, C_K), onesq_ref, qn_ref)
    kv = _in_proj(h, win_ref, C_K, C_END)
    kn = _head_rms(kv[:, 0:D_KV], onesk_ref, kn_ref)
    vn = kv[:, D_KV:2 * D_KV]
    z = _in_proj(h, win_ref, C_Z, C_XBC)
    yield
    smask = smask_ref[...]
    dskip = dsk_ref[...]
    band = WINDOW + CHUNK
    kds = _dup_halves(jnp.concatenate([kbuf[...], kn], axis=0))
    vds = _dup_halves(jnp.concatenate([vbuf[...], vn], axis=0))
    res["k_last"] = kn[tile - WINDOW:tile]
    res["v_last"] = vn[tile - WINDOW:tile]
    kbuf[...] = res["k_last"]
    vbuf[...] = res["v_last"]
    state = {"s2": s2_ref[...]}
    part = {}
    items = []
    for c in range(n_chunks):
        r = slice(c * CHUNK, (c + 1) * CHUNK)
        kr = slice(c * CHUNK, c * CHUNK + band)
        first_valid = (WINDOW // CHUNK - (t * n_chunks + c)) * CHUNK
        items.append(_ssd_steps(part, ("y", c), acs[r], xs[r], dt[r], bm[r], cm[r], state, dskip,
                                smask, CHUNK))
        for g in range(2):
            items.append(_attn_steps(part, ("o", c, g), qn[r, 256 * g:256 * (g + 1)], kds[g][kr],
                                     vds[g][kr], bias_ref[g], sink_ref[g], first_valid, CHUNK))
    yield from _run_staggered(items)
    s2_ref[...] = state["s2"]
    y_ssd = _rms(jnp.concatenate([part["y", c] for c in range(n_chunks)], axis=0) * _silu(z),
                 ssdn_ref[...])
    o = jnp.concatenate([jnp.concatenate([part["o", c, 0], part["o", c, 1]], axis=1)
                         for c in range(n_chunks)], axis=0)
    yield
    mix = (_dot(y_ssd.astype(BF16), wout_ref[0:D_SSD, :])
           + _dot(o.astype(BF16), wout_ref[D_SSD:D_SSD + D_ATTN, :]))
    res["x_mid"] = x + g1 * mix


def _layer_prompt_kernel(x_ref, modm_ref, modf_ref,
                         nmix_ref, win_ref, cw_ref, cb_ref, dtb_ref, alog_ref, dsk_ref, ssdn_ref,
                         qn_ref, kn_ref, sink_ref, bias_ref, tril_ref, onesq_ref, onesk_ref,
                         smask_ref, wout_ref,
                         nffn_ref, wup_ref, fcw_ref, fcb_ref, wdn_ref,
                         xo_ref, ssm_ref, ctail_ref, ko_ref, vo_ref, ftail_ref,
                         xbc_buf, kbuf, vbuf, s2_ref, xmid_buf, hffn_buf, hist, *, tile, n_tiles):
    s = pl.program_id(0)
    t_mix = lax.rem(s, n_tiles)
    t_ffn = lax.rem(s + n_tiles - 1, n_tiles)
    slot_w = lax.rem(s, 2)
    slot_r = 1 - slot_w

    @pl.when(s == 0)
    def _():
        xmid_buf[...] = jnp.zeros(xmid_buf.shape, F32)
        hffn_buf[...] = jnp.zeros(hffn_buf.shape, BF16)

    @pl.when(t_mix == 0)
    def _():
        xbc_buf[...] = jnp.zeros((HIST, D_XBC), F32)
        kbuf[...] = jnp.zeros((WINDOW, D_KV), F32)
        vbuf[...] = jnp.zeros((WINDOW, D_KV), F32)
        s2_ref[...] = jnp.zeros((2 * D_STATE, D_SSD), F32)

    @pl.when(jnp.logical_or(t_ffn == 0, s == 0))
    def _():
        hist[...] = jnp.zeros(hist.shape, F32)

    mres = {}
    modm = modm_ref[0]
    mixer = _mixer_prompt_steps(
        mres, x_ref[0], modm, t_mix, nmix_ref, win_ref, cw_ref, cb_ref, dtb_ref, alog_ref,
        dsk_ref, ssdn_ref, qn_ref, kn_ref, sink_ref, bias_ref, tril_ref, onesq_ref, onesk_ref,
        smask_ref, wout_ref, xbc_buf, kbuf, vbuf, s2_ref, tile)
    def put_out(cols, v):
        xo_ref[0, :, cols] = v

    ffn = _ffn_steps(put_out, xmid_buf[slot_r], hffn_buf[slot_r], modf_ref[0, 5:6, :], hist,
                     wup_ref, fcw_ref, fcb_ref, wdn_ref, hist, None, None, 1, tile)

    def mixer_then_norm():
        yield from mixer
        yield
        xmid_buf[slot_w] = mres["x_mid"]
        hffn_buf[slot_w] = _ffn_norm(mres["x_mid"], modm[3:4], modm[4:5], nffn_ref)

    _run_ordered(STEP_ORDER, {"f": ffn, "m": mixer_then_norm()})

    @pl.when(t_mix == n_tiles - 1)
    def _():
        ssm_ref[0] = _state_rows(s2_ref[...])
        ctail_ref[0] = mres["tail"]
        ko_ref[0] = mres["k_last"]
        vo_ref[0] = mres["v_last"]

    @pl.when(jnp.logical_and(t_ffn == n_tiles - 1, s > 0))
    def _():
        ftail_ref[...] = hist[...]


def _const_spec(shape):
    zeros = (0,) * len(shape)
    return pl.BlockSpec(shape, lambda *_: zeros, pipeline_mode=pl.Buffered(1))


def _layer_spec(a, l):
    zeros = (0,) * (a.ndim - 1)
    return pl.BlockSpec((None,) + a.shape[1:], lambda *_: (l,) + zeros,
                        pipeline_mode=pl.Buffered(1))


def _layer_prompt_call(x, mod, lw, consts, l, tile):
    b, length, _ = x.shape
    n_tiles = length // tile
    n_steps = b * n_tiles
    band = WINDOW + CHUNK
    kern = functools.partial(_layer_prompt_kernel, tile=tile, n_tiles=n_tiles)

    def mix_tile(s):
        return jnp.minimum(s, n_steps - 1)

    def ffn_tile(s):
        return jnp.maximum(s - 1, 0)

    lay = lambda a: (a, _layer_spec(a, l))
    con = lambda a: (a, _const_spec(a.shape))
    params = [lay(lw["norm_mix"]), lay(lw["w_in"]), lay(lw["conv_w"]), lay(lw["conv_b"]),
              lay(lw["dt_bias"]), lay(lw["a_log"]), lay(lw["d_skip"]), lay(lw["ssd_norm"]),
              lay(lw["q_norm"]), lay(lw["k_norm"]), lay(lw["sink_p"]), con(consts["bias_p"]),
              con(consts["tril_p"]), con(consts["ones_q"]), con(consts["ones_k"]),
              con(consts["smask"]), lay(lw["w_out"]),
              lay(lw["norm_ffn"]), lay(lw["w_up"]), lay(lw["conv_ffn_w"]), lay(lw["conv_ffn_b"]),
              lay(lw["w_down"])]
    assert consts["bias_p"].shape == (2, band, 4 * CHUNK)
    return pl.pallas_call(
        kern,
        grid=(n_steps + 1,),
        in_specs=[
            pl.BlockSpec((1, tile, D_MODEL),
                         lambda s: (mix_tile(s) // n_tiles, mix_tile(s) % n_tiles, 0)),
            pl.BlockSpec((None, 1, 8, D_MODEL), lambda s: (l, mix_tile(s) // n_tiles, 0, 0)),
            pl.BlockSpec((None, 1, 8, D_MODEL), lambda s: (l, ffn_tile(s) // n_tiles, 0, 0)),
        ] + [spec for _, spec in params],
        out_specs=[
            pl.BlockSpec((1, tile, D_MODEL),
                         lambda s: (ffn_tile(s) // n_tiles, ffn_tile(s) % n_tiles, 0)),
            pl.BlockSpec((1, D_SSD, D_STATE), lambda s: (mix_tile(s) // n_tiles, 0, 0)),
            pl.BlockSpec((1, HIST, D_XBC), lambda s: (mix_tile(s) // n_tiles, 0, 0)),
            pl.BlockSpec((1, WINDOW, D_KV), lambda s: (mix_tile(s) // n_tiles, 0, 0)),
            pl.BlockSpec((1, WINDOW, D_KV), lambda s: (mix_tile(s) // n_tiles, 0, 0)),
            pl.BlockSpec((1, HIST, 2 * D_FF), lambda s: (ffn_tile(s) // n_tiles, 0, 0)),
        ],
        out_shape=[
            jax.ShapeDtypeStruct((b, length, D_MODEL), F32),
            jax.ShapeDtypeStruct((b, D_SSD, D_STATE), F32),
            jax.ShapeDtypeStruct((b, HIST, D_XBC), F32),
            jax.ShapeDtypeStruct((b, WINDOW, D_KV), F32),
            jax.ShapeDtypeStruct((b, WINDOW, D_KV), F32),
            jax.ShapeDtypeStruct((b, HIST, 2 * D_FF), F32),
        ],
        scratch_shapes=[
            pltpu.VMEM((HIST, D_XBC), F32),
            pltpu.VMEM((WINDOW, D_KV), F32),
            pltpu.VMEM((WINDOW, D_KV), F32),
            pltpu.VMEM((2 * D_STATE, D_SSD), F32),
            pltpu.VMEM((2, tile, D_MODEL), F32),
            pltpu.VMEM((2, tile, D_MODEL), BF16),
            pltpu.VMEM((1, HIST, 2 * D_FF), F32),
        ],
        compiler_params=pltpu.CompilerParams(
            dimension_semantics=("arbitrary",), vmem_limit_bytes=VMEM_LIMIT),
        name="layer_prompt",
    )(x, mod, mod, *[a for a, _ in params])


def _seq_rows(mod_ref, k, length):
    v = mod_ref[k]
    n_seq, d = v.shape
    return jnp.broadcast_to(v[:, None, :], (n_seq, length, d)).reshape(n_seq * length, d)


def _mixer_sample_kernel(x_ref, mod_ref, ssm_in_ref, ccache_ref, kc_ref, vc_ref,
                         nmix_ref, win_ref, cw_ref, cb_ref, dtb_ref, alog_ref,
                         dsk_ref, ssdn_ref, qn_ref, kn_ref, sink_ref, bias_ref, tril_ref,
                         onesq_ref, onesk_ref, smask_ref, wout_ref,
                         xo_ref, ssm_ref, ctail_ref, ko_ref, vo_ref,
                         xbc_buf, *, n_seq, length):
    rows = n_seq * length
    x = x_ref[...]
    h = (_rms(x, nmix_ref[...]) * (1.0 + _seq_rows(mod_ref, 1, length))
         + _seq_rows(mod_ref, 0, length)).astype(BF16)

    xbc_raw = _in_proj(h, win_ref, C_XBC, C_DT)
    xbc_buf[:, 0:HIST, :] = ccache_ref[...]
    conv = _causal_conv(xbc_buf, xbc_raw.reshape(n_seq, length, D_XBC), cw_ref, cb_ref, SSD_CONV)
    ctail_ref[...] = xbc_buf[:, length:length + HIST, :]
    xbc = _silu(conv.reshape(rows, D_XBC))
    xs, bm, cm = xbc[:, 0:D_SSD], xbc[:, D_SSD:D_SSD + D_BC], xbc[:, D_SSD + D_BC:D_XBC]
    dt, acs = _decay_terms(h, win_ref, dtb_ref, alog_ref, tril_ref)
    qn = _head_rms(_in_proj(h, win_ref, C_Q, C_K), onesq_ref, qn_ref)
    kv = _in_proj(h, win_ref, C_K, C_END)
    kn = _head_rms(kv[:, 0:D_KV], onesk_ref, kn_ref)
    vn = kv[:, D_KV:2 * D_KV]
    smask = smask_ref[...]
    dskip = dsk_ref[...]
    part = {}

    def ssd_item(i, r):
        state = {"s2": ssm_in_ref[i]}
        yield from _ssd_steps(part, ("y", i), acs[r], xs[r], dt[r], bm[r], cm[r], state, dskip,
                              smask, length)
        ssm_ref[i] = _state_rows(state["s2"])

    def attn_item(i, r):
        kband = jnp.concatenate([kc_ref[i], kn[r]], axis=0)
        vband = jnp.concatenate([vc_ref[i], vn[r]], axis=0)
        ko_ref[i] = kband[length:length + WINDOW]
        vo_ref[i] = vband[length:length + WINDOW]
        kds, vds = _dup_halves(kband), _dup_halves(vband)
        groups = [_attn_steps(part, ("o", i, g), qn[r, 256 * g:256 * (g + 1)], kds[g], vds[g],
                              bias_ref[g], sink_ref[g], None, length) for g in range(2)]
        yield from _run_staggered(groups, stagger=False)

    items = []
    for i in range(n_seq):
        r = slice(i * length, (i + 1) * length)
        items += [ssd_item(i, r), attn_item(i, r)]
    _drain(_run_staggered(items))
    y = jnp.concatenate([part["y", i] for i in range(n_seq)], axis=0)
    o = jnp.concatenate([jnp.concatenate([part["o", i, 0], part["o", i, 1]], axis=1)
                         for i in range(n_seq)], axis=0)

    z = _in_proj(h, win_ref, C_Z, C_XBC)
    y_ssd = _rms(y * _silu(z), ssdn_ref[...])
    mix = (_dot(y_ssd.astype(BF16), wout_ref[0:D_SSD, :])
           + _dot(o.astype(BF16), wout_ref[D_SSD:D_SSD + D_ATTN, :]))
    xo_ref[...] = x + _seq_rows(mod_ref, 2, length) * mix


def _lead_spec(shape):
    tail_zeros = (0,) * (len(shape) - 1)
    return pl.BlockSpec(shape, lambda i: (i,) + tail_zeros)


def _mod_spec(n_blk, l):
    return pl.BlockSpec((None, 6, n_blk, D_MODEL), lambda i: (l, 0, i, 0))


def _seq_spec(a, n_blk, l):
    zeros = (0,) * (a.ndim - 2)
    return pl.BlockSpec((None, n_blk) + a.shape[2:], lambda i: (l, i) + zeros)


def _mixer_sample_call(x2, mod, ssm_in, ccache, kc, vc, lw, consts, l, n_blk, length):
    n_seq = ssm_in.shape[1]
    rows = n_blk * length
    kern = functools.partial(_mixer_sample_kernel, n_seq=n_blk, length=length)
    seq_args = (ssm_in, ccache, kc, vc)
    lay = lambda a: (a, _layer_spec(a, l))
    con = lambda a: (a, _const_spec(a.shape))
    params = [lay(lw["norm_mix"]), lay(lw["w_in"]), lay(lw["conv_w"]), lay(lw["conv_b"]),
              lay(lw["dt_bias"]), lay(lw["a_log"]), lay(lw["d_skip"]), lay(lw["ssd_norm"]),
              lay(lw["q_norm"]), lay(lw["k_norm"]), lay(lw["sink_s"]), con(consts["bias_s"]),
              con(consts["tril_s"]), con(consts["ones_q"]), con(consts["ones_k"]),
              con(consts["smask"]), lay(lw["w_out"])]
    out_shape = [
        jax.ShapeDtypeStruct((n_seq * length, D_MODEL), F32),
        jax.ShapeDtypeStruct((n_seq, D_SSD, D_STATE), F32),
        jax.ShapeDtypeStruct((n_seq, HIST, D_XBC), F32),
        jax.ShapeDtypeStruct((n_seq, WINDOW, D_KV), F32),
        jax.ShapeDtypeStruct((n_seq, WINDOW, D_KV), F32),
    ]
    return pl.pallas_call(
        kern,
        grid=(n_seq // n_blk,),
        in_specs=([_lead_spec((rows, D_MODEL)), _mod_spec(n_blk, l)]
                  + [_seq_spec(a, n_blk, l) for a in seq_args]
                  + [spec for _, spec in params]),
        out_specs=([_lead_spec((rows, D_MODEL))]
                   + [_lead_spec((n_blk,) + s.shape[1:]) for s in out_shape[1:]]),
        out_shape=out_shape,
        scratch_shapes=[
            pltpu.VMEM((n_blk, HIST + length, D_XBC), F32),
        ],
        compiler_params=pltpu.CompilerParams(
            dimension_semantics=("arbitrary",), vmem_limit_bytes=VMEM_LIMIT),
        name="mixer_sample",
    )(x2, mod, *seq_args, *[a for a, _ in params])


def _ffn_sample_kernel(x_ref, mod_ref, hist_ref, nffn_ref, wup_ref, cw_ref, cb_ref,
                       wdn_ref, xo_ref, tail_ref, ubuf, gbuf, *, n_seq, length):
    x = x_ref[...]
    h = _ffn_norm(x, _seq_rows(mod_ref, 3, length), _seq_rows(mod_ref, 4, length), nffn_ref)
    def put_out(cols, v):
        xo_ref[:, cols] = v

    _drain(_ffn_steps(put_out, x, h, _seq_rows(mod_ref, 5, length), hist_ref, wup_ref, cw_ref,
                      cb_ref, wdn_ref, tail_ref, ubuf, gbuf, n_seq, length))


def _ffn_sample_call(x2, mod, hist, lw, l, n_blk, length):
    n_seq = hist.shape[1]
    rows = n_blk * length
    kern = functools.partial(_ffn_sample_kernel, n_seq=n_blk, length=length)
    const_args = (lw["norm_ffn"], lw["w_up"], lw["conv_ffn_w"], lw["conv_ffn_b"], lw["w_down"])
    out_shape = [
        jax.ShapeDtypeStruct((n_seq * length, D_MODEL), F32),
        jax.ShapeDtypeStruct((n_seq, HIST, 2 * D_FF), F32),
    ]
    return pl.pallas_call(
        kern,
        grid=(n_seq // n_blk,),
        in_specs=([_lead_spec((rows, D_MODEL)), _mod_spec(n_blk, l), _seq_spec(hist, n_blk, l)]
                  + [_layer_spec(a, l) for a in const_args]),
        out_specs=[_lead_spec((rows, D_MODEL)), _lead_spec((n_blk, HIST, 2 * D_FF))],
        out_shape=out_shape,
        scratch_shapes=[
            pltpu.VMEM((n_blk, HIST + length, FF_BLK), F32),
            pltpu.VMEM((n_blk, HIST + length, FF_BLK), F32),
        ],
        compiler_params=pltpu.CompilerParams(
            dimension_semantics=("arbitrary",), vmem_limit_bytes=VMEM_LIMIT),
        name="ffn_sample",
    )(x2, mod, hist, *const_args)


def _expand_heads(v):
    return jnp.repeat(v.astype(F32), SSD_HEAD_DIM, axis=1)[:, None, :]


def _pad_heads(v):
    return jnp.pad(v.astype(F32), ((0, 0), (0, C_Q - C_DT - SSD_HEADS)))[:, None, :]


def _prep_params(w_in, conv_ssd_w, conv_ssd_b, dt_bias, a_log, d_skip, ssd_norm, q_norm, k_norm,
                 sinks, w_out, norm_mix, norm_ffn, w_up, conv_ffn_w, conv_ffn_b, w_down, q_sample):
    o_dt = D_SSD + D_XBC
    o_q = o_dt + SSD_HEADS
    w_dt = jnp.pad(w_in[:, :, o_dt:o_q], ((0, 0), (0, 0), (0, C_Q - C_DT - SSD_HEADS)))
    w_in_r = jnp.concatenate(
        [w_in[:, :, :o_dt], w_dt, w_in[:, :, o_q:]], axis=2).astype(BF16)
    assert w_in_r.shape[2] == C_END
    return {
        "norm_mix": norm_mix[:, None, :],
        "w_in": w_in_r,
        "conv_w": conv_ssd_w,
        "conv_b": conv_ssd_b[:, None, :],
        "dt_bias": _pad_heads(dt_bias),
        "a_log": _pad_heads(a_log),
        "d_skip": _expand_heads(d_skip),
        "ssd_norm": ssd_norm[:, None, :],
        "q_norm": jnp.tile(q_norm, (1, ATTN_HEADS))[:, None, :],
        "k_norm": jnp.tile(k_norm, (1, D_KV // HEAD_DIM))[:, None, :],
        "sink_p": _sink_rows(sinks, CHUNK),
        "sink_s": _sink_rows(sinks, q_sample),
        "w_out": w_out.astype(BF16),
        "norm_ffn": norm_ffn[:, None, :],
        "w_up": w_up.astype(BF16),
        "conv_ffn_w": conv_ffn_w,
        "conv_ffn_b": conv_ffn_b[:, None, :],
        "w_down": w_down.astype(BF16),
    }


def _block_tril(n, blk):
    i = np.arange(n)
    return jnp.asarray(((i[:, None] // blk) == (i[None, :] // blk)) & (i[None, :] <= i[:, None]),
                       dtype=BF16)


def _block_ones(n, blk):
    i = np.arange(n)
    return jnp.asarray((i[:, None] // blk) == (i[None, :] // blk), dtype=BF16)


def _alibi_bias(q):
    slopes = (2.0 ** (-8.0 * np.arange(1, ATTN_HEADS + 1) / ATTN_HEADS)).astype(np.float32)
    dist = np.abs(np.arange(q)[:, None] + WINDOW - np.arange(WINDOW + q)[None, :]).astype(np.float32)
    bias = slopes[:, None, None] * dist[None]
    return jnp.asarray(np.transpose(bias.reshape(2, 4 * q, WINDOW + q), (0, 2, 1)))


def _sink_rows(sinks, q):
    return jnp.repeat(sinks.astype(F32), q, axis=1).reshape(sinks.shape[0], 2, 1, 4 * q)


def _state_mask():
    r = np.arange(2 * D_STATE)[:, None] // D_STATE
    c = np.arange(D_SSD)[None, :] // (D_SSD // 2)
    return jnp.asarray(r == c, dtype=F32)


def _state_to_s2(state):
    lead = state.shape[:2]
    st = jnp.transpose(state, (0, 1, 4, 2, 3)).reshape(lead + (D_STATE, D_SSD))
    half = D_SSD // 2
    zero = jnp.zeros(lead + (D_STATE, half), state.dtype)
    top = jnp.concatenate([st[..., :half], zero], axis=-1)
    bot = jnp.concatenate([zero, st[..., half:]], axis=-1)
    return jnp.concatenate([top, bot], axis=-2)


def _pad_hist(cache):
    k = cache.shape[2]
    return jnp.pad(cache, ((0, 0), (0, 0), (HIST - k, 0), (0, 0)))


def kernel(x_prompt, x_sample, c_prompt, c_sample, state_ssm, cache_conv_ssd, cache_attn_k,
           cache_attn_v, cache_conv_ffn, w_ada, b_ada, norm_mix, w_in, conv_ssd_w, conv_ssd_b,
           dt_bias, a_log, d_skip, ssd_norm, q_norm, k_norm, sinks, w_out, norm_ffn, w_up,
           conv_ffn_w, conv_ffn_b, w_down):
    depth = w_ada.shape[0]
    bp, seq, _ = x_prompt.shape
    bs, lsamp, _ = x_sample.shape
    rows_s = bs * lsamp
    tile = TILE_T
    n_blk = SAMPLE_SEQS

    mod_all = _ada_call(jnp.concatenate([c_prompt, c_sample], axis=0), w_ada, b_ada)
    mod_all = mod_all.reshape(depth, bp + bs, 6, D_MODEL)
    mod_p = jnp.pad(mod_all[:, :bp], ((0, 0), (0, 0), (0, 2), (0, 0)))
    mod_s = jnp.transpose(mod_all[:, bp:], (0, 2, 1, 3))

    consts = {
        "tril_p": _block_tril(tile, CHUNK),
        "tril_s": _block_tril(n_blk * lsamp, lsamp),
        "ones_q": _block_ones(MXU_TILE, HEAD_DIM),
        "ones_k": _block_ones(D_KV, HEAD_DIM),
        "smask": _state_mask(),
        "bias_p": _alibi_bias(CHUNK),
        "bias_s": _alibi_bias(lsamp),
    }

    lw = _prep_params(w_in, conv_ssd_w, conv_ssd_b, dt_bias, a_log, d_skip, ssd_norm, q_norm, k_norm,
                      sinks, w_out, norm_mix, norm_ffn, w_up, conv_ffn_w, conv_ffn_b, w_down, lsamp)
    ssm_in = _state_to_s2(state_ssm)
    conv_in = _pad_hist(cache_conv_ssd)
    ffn_in = _pad_hist(cache_conv_ffn)
    k_in = cache_attn_k.reshape(depth, bs, WINDOW, D_KV)
    v_in = cache_attn_v.reshape(depth, bs, WINDOW, D_KV)

    xp = x_prompt
    xs = x_sample.reshape(rows_s, D_MODEL)
    outs = [[] for _ in range(10)]
    for l in range(depth):
        xp, ssm_p, ct_p, k_p, v_p, ft_p = _layer_prompt_call(xp, mod_p, lw, consts, l, tile)
        xs, ssm_s, ct_s, k_s, v_s = _mixer_sample_call(
            xs, mod_s, ssm_in, conv_in, k_in, v_in, lw, consts, l, n_blk, lsamp)
        xs, ft_s = _ffn_sample_call(xs, mod_s, ffn_in, lw, l, n_blk, lsamp)

        kv_shape = (WINDOW, D_KV // HEAD_DIM, HEAD_DIM)
        for lst, val in zip(outs, (
                ssm_p.reshape(bp, SSD_HEADS, SSD_HEAD_DIM, D_STATE),
                ssm_s.reshape(bs, SSD_HEADS, SSD_HEAD_DIM, D_STATE),
                ct_p[:, HIST - (SSD_CONV - 1):], ct_s[:, HIST - (SSD_CONV - 1):],
                k_p.reshape((bp,) + kv_shape), k_s.reshape((bs,) + kv_shape),
                v_p.reshape((bp,) + kv_shape), v_s.reshape((bs,) + kv_shape),
                ft_p[:, HIST - (FFN_CONV - 1):], ft_s[:, HIST - (FFN_CONV - 1):])):
            lst.append(val)

    return (xp, xs.reshape(bs, lsamp, D_MODEL)) + tuple(jnp.stack(o) for o in outs)
```

```python
import functools

import numpy as np
import jax
import jax.numpy as jnp
from jax import lax
from jax.experimental import pallas as pl
from jax.experimental.pallas import tpu as pltpu

F32 = jnp.float32
BF16 = jnp.bfloat16

D_MODEL = 1024
SSD_HEADS = 8
SSD_HEAD_DIM = 64
D_SSD = 512
D_STATE = 64
D_BC = 128
SSD_CONV = 4
D_XBC = 768
ATTN_HEADS = 8
HEAD_DIM = 64
D_ATTN = 512
D_KV = 128
WINDOW = 128
CHUNK = 64
D_FF = 2816
FFN_CONV = 3
EPS = 1e-6

C_Z, C_XBC, C_DT, C_Q, C_K, C_V, C_END = 0, 512, 1280, 1408, 1920, 2048, 2176

MXU_TILE = 256
HIST = 8
FF_BLK = 256
N_FF_BLK = D_FF // FF_BLK
STEP_ORDER = "f" * 12 + "m" * 4 + "mf" * 8
TILE_T = 256
SAMPLE_SEQS = 16
VMEM_LIMIT = 56 * 1024 * 1024


def _dot(a, b):
    return jnp.dot(a, b, preferred_element_type=F32)


def _dot_nt(a, b):
    return lax.dot_general(a, b, (((1,), (1,)), ((), ())), preferred_element_type=F32)


def _dot_tn(a, b):
    return lax.dot_general(a, b, (((0,), (0,)), ((), ())), preferred_element_type=F32)


def _split_bf16(x, parts):
    out = []
    r = x
    for i in range(parts):
        p = r.astype(BF16)
        out.append(p)
        if i + 1 < parts:
            r = r - p.astype(F32)
    return out


def _dot_exact_rhs(a_bf16, x, parts):
    ps = _split_bf16(x, parts)
    n = x.shape[1]
    acc = None
    while ps:
        if 2 * n == MXU_TILE and len(ps) >= 2:
            both = _dot(a_bf16, jnp.concatenate(ps[:2], axis=1))
            d, ps = both[:, :n] + both[:, n:], ps[2:]
        else:
            d, ps = _dot(a_bf16, ps[0]), ps[1:]
        acc = d if acc is None else acc + d
    return acc


def _dot_exact_lhs(x, b_bf16, parts):
    acc = None
    for p in _split_bf16(x, parts):
        d = _dot(p, b_bf16)
        acc = d if acc is None else acc + d
    return acc


def _rms(x, g):
    ms = jnp.mean(x * x, axis=-1, keepdims=True)
    return x * lax.rsqrt(ms + EPS) * g


def _silu(x):
    return x * (1.0 / (1.0 + jnp.exp2(x * (-np.log2(np.e)))))


def _softplus(x):
    return jnp.maximum(x, 0.0) + jnp.log1p(jnp.exp(-jnp.abs(x)))


def _lane_lo(shape):
    return (lax.broadcasted_iota(jnp.int32, shape, len(shape) - 1) % 128) < 64


def _causal_conv(buf_ref, raw3, w_ref, b_ref, taps):
    length = raw3.shape[1]
    buf_ref[:, HIST:HIST + length, :] = raw3
    y = b_ref[...]
    for k in range(taps):
        lo = HIST - (taps - 1) + k
        y = y + buf_ref[:, lo:lo + length, :] * w_ref[k:k + 1, :]
    return y


def _causal_conv_rolled(raw, hist, w_ref, b_ref, taps):
    row = lax.broadcasted_iota(jnp.int32, hist.shape, 0)
    y = b_ref[...]
    for k in range(taps):
        s = taps - 1 - k
        if s == 0:
            term = raw
        else:
            rolled = pltpu.roll(raw, s, axis=0)
            head = jnp.where(row < s, pltpu.roll(hist, s, axis=0), rolled[0:HIST])
            term = jnp.concatenate([head, rolled[HIST:]], axis=0)
        y = y + term * w_ref[k:k + 1, :]
    return y


def _ssd_steps(res, key, acs, xs, dt, bm, cm, state, dskip, smask, q):
    pad = CHUNK - q

    def stack2(a, b):
        if pad:
            zero = jnp.zeros((pad, a.shape[1]), a.dtype)
            return jnp.concatenate([a, zero, b, zero], axis=0)
        return jnp.concatenate([a, b], axis=0)

    lane = lax.broadcasted_iota(jnp.int32, (q, D_SSD), 1)
    row = lax.broadcasted_iota(jnp.int32, (q, D_SSD), 0)
    rowf = jnp.sum(jnp.where((lane % CHUNK) == row, acs, 0.0), axis=0, keepdims=True)
    a_end = acs[q - 1:q, :]
    xdt = xs * dt
    xw = (xdt * jnp.exp(a_end - acs)).astype(BF16)
    cm_b = cm.astype(BF16)
    zrow = jnp.zeros((2 * D_STATE - q, D_BC), F32)
    bm_t = jnp.concatenate([bm, zrow], axis=0).T.astype(BF16)
    xw_pad = jnp.concatenate([xw, jnp.zeros((2 * D_STATE - q, D_SSD), BF16)], axis=0)
    lo128 = _lane_lo((q, 128))
    bstack = stack2(bm, bm).astype(BF16)
    yield
    s_add = _dot(bm_t, xw_pad) * smask
    cbs = [_dot_nt(jnp.where(lo128 if g == 0 else ~lo128, cm, 0.0).astype(BF16), bstack)
           for g in range(2)]
    yield
    lane128 = lax.broadcasted_iota(jnp.int32, (q, 128), 1)
    row128 = lax.broadcasted_iota(jnp.int32, (q, 128), 0)
    causal = (lane128 % CHUNK) <= row128
    ms, rhss = [], []
    for j in range(4):
        seg = acs[:, 128 * j:128 * (j + 1)] - rowf[:, 128 * j:128 * (j + 1)]
        decay = jnp.exp(jnp.where(causal, seg, -jnp.inf))
        ms.append((cbs[j // 2] * decay).astype(BF16))
        xp = xdt[:, 128 * j:128 * (j + 1)]
        rhss.append(stack2(jnp.where(lo128, xp, 0.0), jnp.where(lo128, 0.0, xp)).astype(BF16))
    yield
    s2 = state["s2"]
    y_off = _dot(cm_b, s2.astype(BF16)) * jnp.exp(acs)
    state["s2"] = s2 * jnp.exp(a_end) + s_add
    y_diag = jnp.concatenate([_dot(m, r) for m, r in zip(ms, rhss)], axis=1)
    res[key] = y_diag + y_off + dskip * xs


def _dup_halves(x):
    lo = _lane_lo(x.shape)
    sw = pltpu.roll(x, 64, axis=1)
    return jnp.where(lo, x, sw).astype(BF16), jnp.where(lo, sw, x).astype(BF16)


def _attn_steps(res, key, qg, kd, vd, bias, sink, first_valid, q):
    band = kd.shape[0]
    lo_q = _lane_lo((q, 128))
    p0, p1 = qg[:, 0:128], qg[:, 128:256]
    qs = jnp.concatenate([jnp.where(lo_q, p0, 0.0), jnp.where(lo_q, 0.0, p0),
                          jnp.where(lo_q, p1, 0.0), jnp.where(lo_q, 0.0, p1)], axis=0)
    s = _dot_nt(kd, qs.astype(BF16))
    yield
    s = s - bias
    if first_valid is not None:
        s = jnp.where(lax.broadcasted_iota(jnp.int32, (band, 1), 0) >= first_valid, s, -jnp.inf)
    m = jnp.maximum(jnp.max(s, axis=0, keepdims=True), sink)
    p = jnp.exp(s - m)
    p = (p / (jnp.sum(p, axis=0, keepdims=True) + jnp.exp(sink - m))).astype(BF16)
    yield
    og = _dot_tn(vd, p).T if 4 * q == MXU_TILE else _dot_tn(p, vd)
    yield
    res[key] = jnp.concatenate([jnp.where(lo_q, og[0:q], og[q:2 * q]),
                                jnp.where(lo_q, og[2 * q:3 * q], og[3 * q:4 * q])], axis=1)


def _run_staggered(gens, stagger=True):
    live, started = [], 0
    while started < len(gens) or live:
        while started < len(gens):
            live.append(gens[started])
            started += 1
            if stagger:
                break
        for g in list(live):
            try:
                next(g)
            except StopIteration:
                live.remove(g)
        yield


def _drain(gen):
    for _ in gen:
        pass


def _state_rows(s2):
    s_t = s2.T
    half = D_SSD // 2
    return jnp.concatenate([s_t[0:half, 0:D_STATE], s_t[half:D_SSD, D_STATE:2 * D_STATE]], axis=0)


def _head_rms(x, ones_ref, g_ref):
    w = ones_ref.shape[0]
    ones = ones_ref[...]
    sq = x * x
    n = x.shape[1]
    if 2 * n == w:
        both = _dot(jnp.concatenate(_split_bf16(sq, 2), axis=1), ones)
        ssq = both[:, :n] + both[:, n:]
    else:
        ssq = jnp.concatenate([_dot_exact_lhs(sq[:, i:i + w], ones, 2) for i in range(0, n, w)],
                              axis=1)
    return x * lax.rsqrt(ssq * (1.0 / HEAD_DIM) + EPS) * g_ref[...]


def _in_proj(h, win_ref, lo, hi):
    return _dot(h, win_ref[:, lo:hi])


def _expand_head_lanes(v):
    n = v.shape[0]
    lo = _lane_lo((n, 128))
    pairs = []
    for j in range(SSD_HEADS // 2):
        even = jnp.broadcast_to(v[:, 2 * j:2 * j + 1], (n, 128))
        odd = jnp.broadcast_to(v[:, 2 * j + 1:2 * j + 2], (n, 128))
        pairs.append(jnp.where(lo, even, odd))
    return jnp.concatenate(pairs, axis=1)


def _decay_terms(h, win_ref, dtb_ref, alog_ref, tril_ref):
    dt = _softplus(_in_proj(h, win_ref, C_DT, C_Q) + dtb_ref[...])
    acs = _dot_exact_rhs(tril_ref[...], dt * (-jnp.exp(alog_ref[...])), 3)
    return _expand_head_lanes(dt), _expand_head_lanes(acs)


def _run_ordered(order, gens):
    for who in order:
        next(gens[who], None)
    for g in gens.values():
        _drain(g)


def _ada_kernel(c_ref, w_ref, b_ref, o_ref):
    c = c_ref[...]
    o_ref[0] = _dot(_silu(c).astype(BF16), w_ref[0].astype(BF16)) + b_ref[0]


def _ada_call(c_all, w_ada, b_ada):
    depth = w_ada.shape[0]
    n = c_all.shape[0]
    blk = 2048
    return pl.pallas_call(
        _ada_kernel,
        grid=(depth, 6 * D_MODEL // blk),
        in_specs=[
            pl.BlockSpec((n, D_MODEL), lambda l, j: (0, 0)),
            pl.BlockSpec((1, D_MODEL, blk), lambda l, j: (l, 0, j)),
            pl.BlockSpec((1, 1, blk), lambda l, j: (l, 0, j)),
        ],
        out_specs=pl.BlockSpec((1, n, blk), lambda l, j: (l, 0, j)),
        out_shape=jax.ShapeDtypeStruct((depth, n, 6 * D_MODEL), F32),
        compiler_params=pltpu.CompilerParams(
            dimension_semantics=("arbitrary", "arbitrary"), vmem_limit_bytes=VMEM_LIMIT),
        name="adaln",
    )(c_all, w_ada, b_ada.reshape(depth, 1, 6 * D_MODEL))


def _mod_norm(x, sh, sc, g_ref):
    ms = jnp.mean(x * x, axis=-1, keepdims=True)
    return (x * lax.rsqrt(ms + EPS) * (g_ref[...] * (1.0 + sc)) + sh).astype(BF16)


def _ffn_norm(x, sh, sc, nffn_ref):
    return _mod_norm(x, sh, sc, nffn_ref)


def _ffn_front_steps(put_act, h, hist_ref, wup_ref, cw_ref, cb_ref, tail_ref, ubuf, gbuf, n_seq,
                     length):
    rows = n_seq * length

    def up(j):
        return [_dot(h, wup_ref[:, base + j * FF_BLK:base + (j + 1) * FF_BLK]) for base in (0, D_FF)]

    raws = up(0)
    yield
    for j in range(N_FF_BLK):
        nxt = up(j + 1) if j + 1 < N_FF_BLK else None
        halves = []
        for raw, buf, base in zip(raws, (ubuf, gbuf), (0, D_FF)):
            lo = base + j * FF_BLK
            cw, cb = cw_ref.at[:, lo:lo + FF_BLK], cb_ref.at[:, lo:lo + FF_BLK]
            if n_seq == 1:
                y = _causal_conv_rolled(raw, hist_ref[0, :, lo:lo + FF_BLK], cw, cb, FFN_CONV)
                tail_ref[0, :, lo:lo + FF_BLK] = raw[length - HIST:length]
            else:
                buf[:, 0:HIST, :] = hist_ref[:, :, lo:lo + FF_BLK]
                y = _causal_conv(buf, raw.reshape(n_seq, length, FF_BLK), cw, cb, FFN_CONV)
                tail_ref[:, :, lo:lo + FF_BLK] = buf[:, length:length + HIST, :]
            halves.append(y.reshape(rows, FF_BLK))
        put_act(j, (_silu(halves[1]) * halves[0]).astype(BF16))
        raws = nxt
        yield


FFN_K_SPLIT = (N_FF_BLK // 2 + 1) * FF_BLK


def _split_acts(acts):
    split = FFN_K_SPLIT // FF_BLK
    return (jnp.concatenate([acts[j] for j in range(split)], axis=1),
            jnp.concatenate([acts[j] for j in range(split, N_FF_BLK)], axis=1))


def _ffn_back_steps(put_out, x, g, act_a, act_b, wdn_ref):
    for n in range(D_MODEL // FF_BLK):
        cols = slice(n * FF_BLK, (n + 1) * FF_BLK)
        d = _dot(act_a, wdn_ref[0:FFN_K_SPLIT, cols])
        yield
        d = d + _dot(act_b, wdn_ref[FFN_K_SPLIT:D_FF, cols])
        put_out(cols, x[:, cols] + g[:, cols] * d)
        yield


def _mixer_prompt_steps(res, x, mod, t, nmix_ref, win_ref, cw_ref, cb_ref, dtb_ref, alog_ref,
                        dsk_ref, ssdn_ref, qn_ref, kn_ref, sink_ref, bias_ref, tril_ref, ones_ref,
                        smask_ref, wout_ref, xbc_buf, kbuf, vbuf, s2_ref, tile):
    n_chunks = tile // CHUNK
    sh1, sc1, g1 = mod[0:1], mod[1:2], mod[2:3]
    h = _mod_norm(x, sh1, sc1, nmix_ref)

    xbc_raw = _in_proj(h, win_ref, C_XBC, C_DT)
    yield
    conv = _causal_conv_rolled(xbc_raw, xbc_buf[...], cw_ref, cb_ref, SSD_CONV)
    res["tail"] = xbc_raw[tile - HIST:tile]
    xbc_buf[...] = res["tail"]
    xbc = _silu(conv)
    xs, bm, cm = xbc[:, 0:D_SSD], xbc[:, D_SSD:D_SSD + D_BC], xbc[:, D_SSD + D_BC:D_XBC]
    yield
    dt, acs = _decay_terms(h, win_ref, dtb_ref, alog_ref, tril_ref)
    yield
    qn = _head_rms(_in_proj(h, win_ref, C_Q, C_K), ones_ref, qn_ref)
    kv = _in_proj(h, win_ref, C_K, C_END)
    kn = _head_rms(kv[:, 0:D_KV], ones_ref, kn_ref)
    vn = kv[:, D_KV:2 * D_KV]
    z = _in_proj(h, win_ref, C_Z, C_XBC)
    yield
    smask = smask_ref[...]
    dskip = dsk_ref[...]
    band = WINDOW + CHUNK
    kds = _dup_halves(jnp.concatenate([kbuf[...], kn], axis=0))
    vds = _dup_halves(jnp.concatenate([vbuf[...], vn], axis=0))
    res["k_last"] = kn[tile - WINDOW:tile]
    res["v_last"] = vn[tile - WINDOW:tile]
    kbuf[...] = res["k_last"]
    vbuf[...] = res["v_last"]
    state = {"s2": s2_ref[...]}
    part = {}
    items = []
    for c in range(n_chunks):
        r = slice(c * CHUNK, (c + 1) * CHUNK)
        kr = slice(c * CHUNK, c * CHUNK + band)
        first_valid = (WINDOW // CHUNK - (t * n_chunks + c)) * CHUNK
        items.append(_ssd_steps(part, ("y", c), acs[r], xs[r], dt[r], bm[r], cm[r], state, dskip,
                                smask, CHUNK))
        for g in range(2):
            items.append(_attn_steps(part, ("o", c, g), qn[r, 256 * g:256 * (g + 1)], kds[g][kr],
                                     vds[g][kr], bias_ref[g], sink_ref[g], first_valid, CHUNK))
    yield from _run_staggered(items)
    s2_ref[...] = state["s2"]
    y_ssd = _rms(jnp.concatenate([part["y", c] for c in range(n_chunks)], axis=0) * _silu(z),
                 ssdn_ref[...])
    o = jnp.concatenate([jnp.concatenate([part["o", c, 0], part["o", c, 1]], axis=1)
                         for c in range(n_chunks)], axis=0)
    yield
    mix = (_dot(y_ssd.astype(BF16), wout_ref[0:D_SSD, :])
           + _dot(o.astype(BF16), wout_ref[D_SSD:D_SSD + D_ATTN, :]))
    res["x_mid"] = x + g1 * mix


def _layer_prompt_kernel(x_ref, modm_ref, modf_ref,
                         nmix_ref, win_ref, cw_ref, cb_ref, dtb_ref, alog_ref, dsk_ref, ssdn_ref,
                         qn_ref, kn_ref, sink_ref, bias_ref, tril_ref, ones_ref,
                         smask_ref, wout_ref,
                         nffn_ref, wup_ref, fcw_ref, fcb_ref, wdn_ref,
                         xo_ref, ssm_ref, ctail_ref, ko_ref, vo_ref, ftail_ref,
                         xbc_buf, kbuf, vbuf, s2_ref, xmid_buf, hffn_buf, hist, *, tile, n_tiles):
    s = pl.program_id(0)
    t_mix = lax.rem(s, n_tiles)
    t_ffn = lax.rem(s + n_tiles - 1, n_tiles)
    slot_w = lax.rem(s, 2)
    slot_r = 1 - slot_w

    @pl.when(s == 0)
    def _():
        xmid_buf[...] = jnp.zeros(xmid_buf.shape, F32)
        hffn_buf[...] = jnp.zeros(hffn_buf.shape, BF16)

    @pl.when(t_mix == 0)
    def _():
        xbc_buf[...] = jnp.zeros((HIST, D_XBC), F32)
        kbuf[...] = jnp.zeros((WINDOW, D_KV), F32)
        vbuf[...] = jnp.zeros((WINDOW, D_KV), F32)
        s2_ref[...] = jnp.zeros((2 * D_STATE, D_SSD), F32)

    @pl.when(jnp.logical_or(t_ffn == 0, s == 0))
    def _():
        hist[...] = jnp.zeros(hist.shape, F32)

    mres = {}
    modm = modm_ref[0]
    mixer = _mixer_prompt_steps(
        mres, x_ref[0], modm, t_mix, nmix_ref, win_ref, cw_ref, cb_ref, dtb_ref, alog_ref,
        dsk_ref, ssdn_ref, qn_ref, kn_ref, sink_ref, bias_ref, tril_ref, ones_ref,
        smask_ref, wout_ref, xbc_buf, kbuf, vbuf, s2_ref, tile)

    def mixer_then_norm():
        yield from mixer
        yield
        xmid_buf[slot_w] = mres["x_mid"]
        hffn_buf[slot_w] = _ffn_norm(mres["x_mid"], modm[3:4], modm[4:5], nffn_ref)

    def put_out(cols, v):
        xo_ref[0, :, cols] = v

    def ffn():
        acts = {}
        yield from _ffn_front_steps(acts.__setitem__, hffn_buf[slot_r], hist, wup_ref, fcw_ref,
                                    fcb_ref, hist, None, None, 1, tile)
        yield from _ffn_back_steps(put_out, xmid_buf[slot_r], modf_ref[0, 5:6, :],
                                   *_split_acts(acts), wdn_ref)

    _run_ordered(STEP_ORDER, {"f": ffn(), "m": mixer_then_norm()})

    @pl.when(t_mix == n_tiles - 1)
    def _():
        ssm_ref[0] = _state_rows(s2_ref[...])
        ctail_ref[0] = mres["tail"]
        ko_ref[0] = mres["k_last"]
        vo_ref[0] = mres["v_last"]

    @pl.when(jnp.logical_and(t_ffn == n_tiles - 1, s > 0))
    def _():
        ftail_ref[...] = hist[...]


def _const_spec(shape):
    zeros = (0,) * len(shape)
    return pl.BlockSpec(shape, lambda *_: zeros, pipeline_mode=pl.Buffered(1))


def _layer_spec(a, l):
    zeros = (0,) * (a.ndim - 1)
    return pl.BlockSpec((None,) + a.shape[1:], lambda *_: (l,) + zeros,
                        pipeline_mode=pl.Buffered(1))


def _layer_prompt_call(x, mod, lw, consts, l, tile):
    b, length, _ = x.shape
    n_tiles = length // tile
    n_steps = b * n_tiles
    band = WINDOW + CHUNK
    kern = functools.partial(_layer_prompt_kernel, tile=tile, n_tiles=n_tiles)

    def mix_tile(s):
        return jnp.minimum(s, n_steps - 1)

    def ffn_tile(s):
        return jnp.maximum(s - 1, 0)

    lay = lambda a: (a, _layer_spec(a, l))
    con = lambda a: (a, _const_spec(a.shape))
    params = [lay(lw["norm_mix"]), lay(lw["w_in"]), lay(lw["conv_w"]), lay(lw["conv_b"]),
              lay(lw["dt_bias"]), lay(lw["a_log"]), lay(lw["d_skip"]), lay(lw["ssd_norm"]),
              lay(lw["q_norm"]), lay(lw["k_norm"]), lay(lw["sink_p"]), con(consts["bias_p"]),
              con(consts["tril_p"]), con(consts["ones"]),
              con(consts["smask"]), lay(lw["w_out"]),
              lay(lw["norm_ffn"]), lay(lw["w_up"]), lay(lw["conv_ffn_w"]), lay(lw["conv_ffn_b"]),
              lay(lw["w_down"])]
    assert consts["bias_p"].shape == (2, band, 4 * CHUNK)
    return pl.pallas_call(
        kern,
        grid=(n_steps + 1,),
        in_specs=[
            pl.BlockSpec((1, tile, D_MODEL),
                         lambda s: (mix_tile(s) // n_tiles, mix_tile(s) % n_tiles, 0)),
            pl.BlockSpec((None, 1, 8, D_MODEL), lambda s: (l, mix_tile(s) // n_tiles, 0, 0)),
            pl.BlockSpec((None, 1, 8, D_MODEL), lambda s: (l, ffn_tile(s) // n_tiles, 0, 0)),
        ] + [spec for _, spec in params],
        out_specs=[
            pl.BlockSpec((1, tile, D_MODEL),
                         lambda s: (ffn_tile(s) // n_tiles, ffn_tile(s) % n_tiles, 0)),
            pl.BlockSpec((1, D_SSD, D_STATE), lambda s: (mix_tile(s) // n_tiles, 0, 0)),
            pl.BlockSpec((1, HIST, D_XBC), lambda s: (mix_tile(s) // n_tiles, 0, 0)),
            pl.BlockSpec((1, WINDOW, D_KV), lambda s: (mix_tile(s) // n_tiles, 0, 0)),
            pl.BlockSpec((1, WINDOW, D_KV), lambda s: (mix_tile(s) // n_tiles, 0, 0)),
            pl.BlockSpec((1, HIST, 2 * D_FF), lambda s: (ffn_tile(s) // n_tiles, 0, 0)),
        ],
        out_shape=[
            jax.ShapeDtypeStruct((b, length, D_MODEL), F32),
            jax.ShapeDtypeStruct((b, D_SSD, D_STATE), F32),
            jax.ShapeDtypeStruct((b, HIST, D_XBC), F32),
            jax.ShapeDtypeStruct((b, WINDOW, D_KV), F32),
            jax.ShapeDtypeStruct((b, WINDOW, D_KV), F32),
            jax.ShapeDtypeStruct((b, HIST, 2 * D_FF), F32),
        ],
        scratch_shapes=[
            pltpu.VMEM((HIST, D_XBC), F32),
            pltpu.VMEM((WINDOW, D_KV), F32),
            pltpu.VMEM((WINDOW, D_KV), F32),
            pltpu.VMEM((2 * D_STATE, D_SSD), F32),
            pltpu.VMEM((2, tile, D_MODEL), F32),
            pltpu.VMEM((2, tile, D_MODEL), BF16),
            pltpu.VMEM((1, HIST, 2 * D_FF), F32),
        ],
        compiler_params=pltpu.CompilerParams(
            dimension_semantics=("arbitrary",), vmem_limit_bytes=VMEM_LIMIT),
        name="layer_prompt",
    )(x, mod, mod, *[a for a, _ in params])


def _seq_rows(mod_ref, k, length):
    v = mod_ref[k]
    n_seq, d = v.shape
    return jnp.broadcast_to(v[:, None, :], (n_seq, length, d)).reshape(n_seq * length, d)


def _mixer_sample_kernel(x_ref, mod_ref, ssm_in_ref, ccache_ref, kc_ref, vc_ref,
                         nmix_ref, win_ref, cw_ref, cb_ref, dtb_ref, alog_ref,
                         dsk_ref, ssdn_ref, qn_ref, kn_ref, sink_ref, bias_ref, tril_ref,
                         ones_ref, smask_ref, wout_ref,
                         xo_ref, ssm_ref, ctail_ref, ko_ref, vo_ref,
                         xbc_buf, *, n_seq, length):
    rows = n_seq * length
    x = x_ref[...]
    h = (_rms(x, nmix_ref[...]) * (1.0 + _seq_rows(mod_ref, 1, length))
         + _seq_rows(mod_ref, 0, length)).astype(BF16)

    xbc_raw = _in_proj(h, win_ref, C_XBC, C_DT)
    xbc_buf[:, 0:HIST, :] = ccache_ref[...]
    conv = _causal_conv(xbc_buf, xbc_raw.reshape(n_seq, length, D_XBC), cw_ref, cb_ref, SSD_CONV)
    ctail_ref[...] = xbc_buf[:, length:length + HIST, :]
    xbc = _silu(conv.reshape(rows, D_XBC))
    xs, bm, cm = xbc[:, 0:D_SSD], xbc[:, D_SSD:D_SSD + D_BC], xbc[:, D_SSD + D_BC:D_XBC]
    dt, acs = _decay_terms(h, win_ref, dtb_ref, alog_ref, tril_ref)
    qn = _head_rms(_in_proj(h, win_ref, C_Q, C_K), ones_ref, qn_ref)
    kv = _in_proj(h, win_ref, C_K, C_END)
    kn = _head_rms(kv[:, 0:D_KV], ones_ref, kn_ref)
    vn = kv[:, D_KV:2 * D_KV]
    smask = smask_ref[...]
    dskip = dsk_ref[...]
    part = {}

    def ssd_item(i, r):
        state = {"s2": ssm_in_ref[i]}
        yield from _ssd_steps(part, ("y", i), acs[r], xs[r], dt[r], bm[r], cm[r], state, dskip,
                              smask, length)
        ssm_ref[i] = _state_rows(state["s2"])

    def attn_item(i, r):
        kband = jnp.concatenate([kc_ref[i], kn[r]], axis=0)
        vband = jnp.concatenate([vc_ref[i], vn[r]], axis=0)
        ko_ref[i] = kband[length:length + WINDOW]
        vo_ref[i] = vband[length:length + WINDOW]
        kds, vds = _dup_halves(kband), _dup_halves(vband)
        groups = [_attn_steps(part, ("o", i, g), qn[r, 256 * g:256 * (g + 1)], kds[g], vds[g],
                              bias_ref[g], sink_ref[g], None, length) for g in range(2)]
        yield from _run_staggered(groups, stagger=False)

    items = []
    for i in range(n_seq):
        r = slice(i * length, (i + 1) * length)
        items += [ssd_item(i, r), attn_item(i, r)]
    _drain(_run_staggered(items))
    y = jnp.concatenate([part["y", i] for i in range(n_seq)], axis=0)
    o = jnp.concatenate([jnp.concatenate([part["o", i, 0], part["o", i, 1]], axis=1)
                         for i in range(n_seq)], axis=0)

    z = _in_proj(h, win_ref, C_Z, C_XBC)
    y_ssd = _rms(y * _silu(z), ssdn_ref[...])
    mix = (_dot(y_ssd.astype(BF16), wout_ref[0:D_SSD, :])
           + _dot(o.astype(BF16), wout_ref[D_SSD:D_SSD + D_ATTN, :]))
    xo_ref[...] = x + _seq_rows(mod_ref, 2, length) * mix


def _lead_spec(shape):
    tail_zeros = (0,) * (len(shape) - 1)
    return pl.BlockSpec(shape, lambda i: (i,) + tail_zeros)


def _mod_spec(n_blk, l):
    return pl.BlockSpec((None, 6, n_blk, D_MODEL), lambda i: (l, 0, i, 0))


def _seq_spec(a, n_blk, l):
    zeros = (0,) * (a.ndim - 2)
    return pl.BlockSpec((None, n_blk) + a.shape[2:], lambda i: (l, i) + zeros)


def _mixer_sample_call(x2, mod, ssm_in, ccache, kc, vc, lw, consts, l, n_blk, length):
    n_seq = ssm_in.shape[1]
    rows = n_blk * length
    kern = functools.partial(_mixer_sample_kernel, n_seq=n_blk, length=length)
    seq_args = (ssm_in, ccache, kc, vc)
    lay = lambda a: (a, _layer_spec(a, l))
    con = lambda a: (a, _const_spec(a.shape))
    params = [lay(lw["norm_mix"]), lay(lw["w_in"]), lay(lw["conv_w"]), lay(lw["conv_b"]),
              lay(lw["dt_bias"]), lay(lw["a_log"]), lay(lw["d_skip"]), lay(lw["ssd_norm"]),
              lay(lw["q_norm"]), lay(lw["k_norm"]), lay(lw["sink_s"]), con(consts["bias_s"]),
              con(consts["tril_s"]), con(consts["ones"]),
              con(consts["smask"]), lay(lw["w_out"])]
    out_shape = [
        jax.ShapeDtypeStruct((n_seq * length, D_MODEL), F32),
        jax.ShapeDtypeStruct((n_seq, D_SSD, D_STATE), F32),
        jax.ShapeDtypeStruct((n_seq, HIST, D_XBC), F32),
        jax.ShapeDtypeStruct((n_seq, WINDOW, D_KV), F32),
        jax.ShapeDtypeStruct((n_seq, WINDOW, D_KV), F32),
    ]
    return pl.pallas_call(
        kern,
        grid=(n_seq // n_blk,),
        in_specs=([_lead_spec((rows, D_MODEL)), _mod_spec(n_blk, l)]
                  + [_seq_spec(a, n_blk, l) for a in seq_args]
                  + [spec for _, spec in params]),
        out_specs=([_lead_spec((rows, D_MODEL))]
                   + [_lead_spec((n_blk,) + s.shape[1:]) for s in out_shape[1:]]),
        out_shape=out_shape,
        scratch_shapes=[
            pltpu.VMEM((n_blk, HIST + length, D_XBC), F32),
        ],
        compiler_params=pltpu.CompilerParams(
            dimension_semantics=("arbitrary",), vmem_limit_bytes=VMEM_LIMIT),
        name="mixer_sample",
    )(x2, mod, *seq_args, *[a for a, _ in params])


def _ffn_sample_kernel(x_ref, mod_ref, hist_ref, nffn_ref, wup_ref, cw_ref, cb_ref,
                       wdn_ref, xo_ref, tail_ref, ubuf, gbuf, *, n_seq, length):
    x = x_ref[...]
    h = _ffn_norm(x, _seq_rows(mod_ref, 3, length), _seq_rows(mod_ref, 4, length), nffn_ref)
    acts = {}

    def put_out(cols, v):
        xo_ref[:, cols] = v

    _drain(_ffn_front_steps(acts.__setitem__, h, hist_ref, wup_ref, cw_ref, cb_ref, tail_ref, ubuf,
                            gbuf, n_seq, length))
    _drain(_ffn_back_steps(put_out, x, _seq_rows(mod_ref, 5, length), *_split_acts(acts),
                           wdn_ref))


def _ffn_sample_call(x2, mod, hist, lw, l, n_blk, length):
    n_seq = hist.shape[1]
    rows = n_blk * length
    kern = functools.partial(_ffn_sample_kernel, n_seq=n_blk, length=length)
    const_args = (lw["norm_ffn"], lw["w_up"], lw["conv_ffn_w"], lw["conv_ffn_b"], lw["w_down"])
    out_shape = [
        jax.ShapeDtypeStruct((n_seq * length, D_MODEL), F32),
        jax.ShapeDtypeStruct((n_seq, HIST, 2 * D_FF), F32),
    ]
    return pl.pallas_call(
        kern,
        grid=(n_seq // n_blk,),
        in_specs=([_lead_spec((rows, D_MODEL)), _mod_spec(n_blk, l), _seq_spec(hist, n_blk, l)]
                  + [_layer_spec(a, l) for a in const_args]),
        out_specs=[_lead_spec((rows, D_MODEL)), _lead_spec((n_blk, HIST, 2 * D_FF))],
        out_shape=out_shape,
        scratch_shapes=[
            pltpu.VMEM((n_blk, HIST + length, FF_BLK), F32),
            pltpu.VMEM((n_blk, HIST + length, FF_BLK), F32),
        ],
        compiler_params=pltpu.CompilerParams(
            dimension_semantics=("arbitrary",), vmem_limit_bytes=VMEM_LIMIT),
        name="ffn_sample",
    )(x2, mod, hist, *const_args)


def _expand_heads(v):
    return jnp.repeat(v.astype(F32), SSD_HEAD_DIM, axis=1)[:, None, :]


def _pad_heads(v):
    return jnp.pad(v.astype(F32), ((0, 0), (0, C_Q - C_DT - SSD_HEADS)))[:, None, :]


def _prep_params(w_in, conv_ssd_w, conv_ssd_b, dt_bias, a_log, d_skip, ssd_norm, q_norm, k_norm,
                 sinks, w_out, norm_mix, norm_ffn, w_up, conv_ffn_w, conv_ffn_b, w_down, q_sample):
    o_dt = D_SSD + D_XBC
    o_q = o_dt + SSD_HEADS
    w_dt = jnp.pad(w_in[:, :, o_dt:o_q], ((0, 0), (0, 0), (0, C_Q - C_DT - SSD_HEADS)))
    w_in_r = jnp.concatenate(
        [w_in[:, :, :o_dt], w_dt, w_in[:, :, o_q:]], axis=2).astype(BF16)
    assert w_in_r.shape[2] == C_END
    return {
        "norm_mix": norm_mix[:, None, :],
        "w_in": w_in_r,
        "conv_w": conv_ssd_w,
        "conv_b": conv_ssd_b[:, None, :],
        "dt_bias": _pad_heads(dt_bias),
        "a_log": _pad_heads(a_log),
        "d_skip": _expand_heads(d_skip),
        "ssd_norm": ssd_norm[:, None, :],
        "q_norm": jnp.tile(q_norm * HEAD_DIM ** -0.5, (1, ATTN_HEADS))[:, None, :],
        "k_norm": jnp.tile(k_norm, (1, D_KV // HEAD_DIM))[:, None, :],
        "sink_p": _sink_rows(sinks, CHUNK),
        "sink_s": _sink_rows(sinks, q_sample),
        "w_out": w_out.astype(BF16),
        "norm_ffn": norm_ffn[:, None, :],
        "w_up": w_up.astype(BF16),
        "conv_ffn_w": conv_ffn_w,
        "conv_ffn_b": conv_ffn_b[:, None, :],
        "w_down": w_down.astype(BF16),
    }


def _block_tril(n, blk):
    i = np.arange(n)
    return jnp.asarray(((i[:, None] // blk) == (i[None, :] // blk)) & (i[None, :] <= i[:, None]),
                       dtype=BF16)


def _block_ones(n, blk):
    i = np.arange(n)
    return jnp.asarray((i[:, None] // blk) == (i[None, :] // blk), dtype=BF16)


def _alibi_bias(q):
    slopes = (2.0 ** (-8.0 * np.arange(1, ATTN_HEADS + 1) / ATTN_HEADS)).astype(np.float32)
    dist = np.abs(np.arange(q)[:, None] + WINDOW - np.arange(WINDOW + q)[None, :]).astype(np.float32)
    bias = slopes[:, None, None] * dist[None]
    return jnp.asarray(np.transpose(bias.reshape(2, 4 * q, WINDOW + q), (0, 2, 1)))


def _sink_rows(sinks, q):
    return jnp.repeat(sinks.astype(F32), q, axis=1).reshape(sinks.shape[0], 2, 1, 4 * q)


def _state_mask():
    r = np.arange(2 * D_STATE)[:, None] // D_STATE
    c = np.arange(D_SSD)[None, :] // (D_SSD // 2)
    return jnp.asarray(r == c, dtype=F32)


def _state_to_s2(state):
    lead = state.shape[:2]
    st = jnp.transpose(state, (0, 1, 4, 2, 3)).reshape(lead + (D_STATE, D_SSD))
    half = D_SSD // 2
    zero = jnp.zeros(lead + (D_STATE, half), state.dtype)
    top = jnp.concatenate([st[..., :half], zero], axis=-1)
    bot = jnp.concatenate([zero, st[..., half:]], axis=-1)
    return jnp.concatenate([top, bot], axis=-2)


def _pad_hist(cache):
    k = cache.shape[2]
    return jnp.pad(cache, ((0, 0), (0, 0), (HIST - k, 0), (0, 0)))


def kernel(x_prompt, x_sample, c_prompt, c_sample, state_ssm, cache_conv_ssd, cache_attn_k,
           cache_attn_v, cache_conv_ffn, w_ada, b_ada, norm_mix, w_in, conv_ssd_w, conv_ssd_b,
           dt_bias, a_log, d_skip, ssd_norm, q_norm, k_norm, sinks, w_out, norm_ffn, w_up,
           conv_ffn_w, conv_ffn_b, w_down):
    depth = w_ada.shape[0]
    bp, seq, _ = x_prompt.shape
    bs, lsamp, _ = x_sample.shape
    rows_s = bs * lsamp
    tile = TILE_T
    n_blk = SAMPLE_SEQS

    mod_all = _ada_call(jnp.concatenate([c_prompt, c_sample], axis=0), w_ada, b_ada)
    mod_all = mod_all.reshape(depth, bp + bs, 6, D_MODEL)
    mod_p = jnp.pad(mod_all[:, :bp], ((0, 0), (0, 0), (0, 2), (0, 0)))
    mod_s = jnp.transpose(mod_all[:, bp:], (0, 2, 1, 3))

    consts = {
        "tril_p": _block_tril(tile, CHUNK),
        "tril_s": _block_tril(n_blk * lsamp, lsamp),
        "ones": _block_ones(MXU_TILE, HEAD_DIM),
        "smask": _state_mask(),
        "bias_p": _alibi_bias(CHUNK),
        "bias_s": _alibi_bias(lsamp),
    }

    lw = _prep_params(w_in, conv_ssd_w, conv_ssd_b, dt_bias, a_log, d_skip, ssd_norm, q_norm, k_norm,
                      sinks, w_out, norm_mix, norm_ffn, w_up, conv_ffn_w, conv_ffn_b, w_down, lsamp)
    ssm_in = _state_to_s2(state_ssm)
    conv_in = _pad_hist(cache_conv_ssd)
    ffn_in = _pad_hist(cache_conv_ffn)
    k_in = cache_attn_k.reshape(depth, bs, WINDOW, D_KV)
    v_in = cache_attn_v.reshape(depth, bs, WINDOW, D_KV)

    xp = x_prompt
    xs = x_sample.reshape(rows_s, D_MODEL)
    outs = [[] for _ in range(10)]
    for l in range(depth):
        xp, ssm_p, ct_p, k_p, v_p, ft_p = _layer_prompt_call(xp, mod_p, lw, consts, l, tile)
        xs, ssm_s, ct_s, k_s, v_s = _mixer_sample_call(
            xs, mod_s, ssm_in, conv_in, k_in, v_in, lw, consts, l, n_blk, lsamp)
        xs, ft_s = _ffn_sample_call(xs, mod_s, ffn_in, lw, l, n_blk, lsamp)

        kv_shape = (WINDOW, D_KV // HEAD_DIM, HEAD_DIM)
        for lst, val in zip(outs, (
                ssm_p.reshape(bp, SSD_HEADS, SSD_HEAD_DIM, D_STATE),
                ssm_s.reshape(bs, SSD_HEADS, SSD_HEAD_DIM, D_STATE),
                ct_p[:, HIST - (SSD_CONV - 1):], ct_s[:, HIST - (SSD_CONV - 1):],
                k_p.reshape((bp,) + kv_shape), k_s.reshape((bs,) + kv_shape),
                v_p.reshape((bp,) + kv_shape), v_s.reshape((bs,) + kv_shape),
                ft_p[:, HIST - (FFN_CONV - 1):], ft_s[:, HIST - (FFN_CONV - 1):])):
            lst.append(val)

    return (xp, xs.reshape(bs, lsamp, D_MODEL)) + tuple(jnp.stack(o) for o in outs)
```

```python
import functools

import numpy as np
import jax
import jax.numpy as jnp
from jax import lax
from jax.experimental import pallas as pl
from jax.experimental.pallas import tpu as pltpu

F32 = jnp.float32
BF16 = jnp.bfloat16

D_MODEL = 1024
SSD_HEADS = 8
SSD_HEAD_DIM = 64
D_SSD = 512
D_STATE = 64
D_BC = 128
SSD_CONV = 4
D_XBC = 768
ATTN_HEADS = 8
HEAD_DIM = 64
D_ATTN = 512
D_KV = 128
WINDOW = 128
CHUNK = 64
D_FF = 2816
FFN_CONV = 3
EPS = 1e-6

C_Z, C_XBC, C_DT, C_Q, C_K, C_V, C_END = 0, 512, 1280, 1408, 1920, 2048, 2176

MXU_TILE = 256
HIST = 8
FF_BLK = 256
N_FF_BLK = D_FF // FF_BLK
STEP_ORDER = "f" * 12 + "m" * 4 + "mf" * 8
TILE_T = 256
SAMPLE_SEQS = 16
VMEM_LIMIT = 56 * 1024 * 1024


def _dot(a, b):
    return jnp.dot(a, b, preferred_element_type=F32)


def _dot_nt(a, b):
    return lax.dot_general(a, b, (((1,), (1,)), ((), ())), preferred_element_type=F32)


def _dot_tn(a, b):
    return lax.dot_general(a, b, (((0,), (0,)), ((), ())), preferred_element_type=F32)


def _split_bf16(x, parts):
    out = []
    r = x
    for i in range(parts):
        p = r.astype(BF16)
        out.append(p)
        if i + 1 < parts:
            r = r - p.astype(F32)
    return out


def _dot_exact_rhs(a_bf16, x, parts):
    ps = _split_bf16(x, parts)
    n = x.shape[1]
    acc = None
    while ps:
        if 2 * n == MXU_TILE and len(ps) >= 2:
            both = _dot(a_bf16, jnp.concatenate(ps[:2], axis=1))
            d, ps = both[:, :n] + both[:, n:], ps[2:]
        else:
            d, ps = _dot(a_bf16, ps[0]), ps[1:]
        acc = d if acc is None else acc + d
    return acc


def _dot_exact_lhs(x, b_bf16, parts):
    acc = None
    for p in _split_bf16(x, parts):
        d = _dot(p, b_bf16)
        acc = d if acc is None else acc + d
    return acc


def _rms(x, g):
    ms = jnp.mean(x * x, axis=-1, keepdims=True)
    return x * lax.rsqrt(ms + EPS) * g


def _silu(x):
    return x * (1.0 / (1.0 + jnp.exp2(x * (-np.log2(np.e)))))


def _softplus(x):
    return jnp.maximum(x, 0.0) + jnp.log1p(jnp.exp(-jnp.abs(x)))


def _lane_lo(shape):
    return (lax.broadcasted_iota(jnp.int32, shape, len(shape) - 1) % 128) < 64


def _causal_conv(buf_ref, raw3, w_ref, b_ref, taps):
    length = raw3.shape[1]
    buf_ref[:, HIST:HIST + length, :] = raw3
    y = b_ref[...]
    for k in range(taps):
        lo = HIST - (taps - 1) + k
        y = y + buf_ref[:, lo:lo + length, :] * w_ref[k:k + 1, :]
    return y


def _causal_conv_rolled(raw, hist, w_ref, b_ref, taps):
    row = lax.broadcasted_iota(jnp.int32, hist.shape, 0)
    y = b_ref[...]
    for k in range(taps):
        s = taps - 1 - k
        if s == 0:
            term = raw
        else:
            rolled = pltpu.roll(raw, s, axis=0)
            head = jnp.where(row < s, pltpu.roll(hist, s, axis=0), rolled[0:HIST])
            term = jnp.concatenate([head, rolled[HIST:]], axis=0)
        y = y + term * w_ref[k:k + 1, :]
    return y


def _ssd_steps(res, key, acs, xs, dt, bm, cm, state, dskip, smask, q):
    pad = CHUNK - q

    def stack2(a, b):
        if pad:
            zero = jnp.zeros((pad, a.shape[1]), a.dtype)
            return jnp.concatenate([a, zero, b, zero], axis=0)
        return jnp.concatenate([a, b], axis=0)

    lane = lax.broadcasted_iota(jnp.int32, (q, D_SSD), 1)
    row = lax.broadcasted_iota(jnp.int32, (q, D_SSD), 0)
    rowf = jnp.sum(jnp.where((lane % CHUNK) == row, acs, 0.0), axis=0, keepdims=True)
    a_end = acs[q - 1:q, :]
    xdt = xs * dt
    xw = (xdt * jnp.exp(a_end - acs)).astype(BF16)
    cm_b = cm.astype(BF16)
    zrow = jnp.zeros((2 * D_STATE - q, D_BC), F32)
    bm_t = jnp.concatenate([bm, zrow], axis=0).T.astype(BF16)
    xw_pad = jnp.concatenate([xw, jnp.zeros((2 * D_STATE - q, D_SSD), BF16)], axis=0)
    lo128 = _lane_lo((q, 128))
    bstack = stack2(bm, bm).astype(BF16)
    yield
    s_add = _dot(bm_t, xw_pad) * smask
    cbs = [_dot_nt(jnp.where(lo128 if g == 0 else ~lo128, cm, 0.0).astype(BF16), bstack)
           for g in range(2)]
    yield
    lane128 = lax.broadcasted_iota(jnp.int32, (q, 128), 1)
    row128 = lax.broadcasted_iota(jnp.int32, (q, 128), 0)
    causal = (lane128 % CHUNK) <= row128
    ms, rhss = [], []
    for j in range(4):
        seg = acs[:, 128 * j:128 * (j + 1)] - rowf[:, 128 * j:128 * (j + 1)]
        decay = jnp.exp(jnp.where(causal, seg, -jnp.inf))
        ms.append((cbs[j // 2] * decay).astype(BF16))
        xp = xdt[:, 128 * j:128 * (j + 1)]
        rhss.append(stack2(jnp.where(lo128, xp, 0.0), jnp.where(lo128, 0.0, xp)).astype(BF16))
    yield
    s2 = state["s2"]
    y_off = _dot(cm_b, s2.astype(BF16)) * jnp.exp(acs)
    state["s2"] = s2 * jnp.exp(a_end) + s_add
    y_diag = jnp.concatenate([_dot(m, r) for m, r in zip(ms, rhss)], axis=1)
    res[key] = y_diag + y_off + dskip * xs


def _dup_halves(x):
    lo = _lane_lo(x.shape)
    sw = pltpu.roll(x, 64, axis=1)
    return jnp.where(lo, x, sw).astype(BF16), jnp.where(lo, sw, x).astype(BF16)


def _attn_steps(res, key, qg, kd, vd, bias, sink, first_valid, q):
    band = kd.shape[0]
    lo_q = _lane_lo((q, 128))
    p0, p1 = qg[:, 0:128], qg[:, 128:256]
    qs = jnp.concatenate([jnp.where(lo_q, p0, 0.0), jnp.where(lo_q, 0.0, p0),
                          jnp.where(lo_q, p1, 0.0), jnp.where(lo_q, 0.0, p1)], axis=0)
    s = _dot_nt(kd, qs.astype(BF16))
    yield
    s = s - bias
    if first_valid is not None:
        s = jnp.where(lax.broadcasted_iota(jnp.int32, (band, 1), 0) >= first_valid, s, -jnp.inf)
    m = jnp.maximum(jnp.max(s, axis=0, keepdims=True), sink)
    p = jnp.exp(s - m)
    p = (p / (jnp.sum(p, axis=0, keepdims=True) + jnp.exp(sink - m))).astype(BF16)
    yield
    og = _dot_tn(p, vd)
    yield
    res[key] = jnp.concatenate([jnp.where(lo_q, og[0:q], og[q:2 * q]),
                                jnp.where(lo_q, og[2 * q:3 * q], og[3 * q:4 * q])], axis=1)


def _run_staggered(gens, stagger=True):
    live, started = [], 0
    while started < len(gens) or live:
        while started < len(gens):
            live.append(gens[started])
            started += 1
            if stagger:
                break
        for g in list(live):
            try:
                next(g)
            except StopIteration:
                live.remove(g)
        yield


def _drain(gen):
    for _ in gen:
        pass


def _state_rows(s2):
    s_t = s2.T
    half = D_SSD // 2
    return jnp.concatenate([s_t[0:half, 0:D_STATE], s_t[half:D_SSD, D_STATE:2 * D_STATE]], axis=0)


def _head_rms(x, ones_ref, g_ref):
    w = ones_ref.shape[0]
    ones = ones_ref[...]
    sq = x * x
    n = x.shape[1]
    if 2 * n == w:
        both = _dot(jnp.concatenate(_split_bf16(sq, 2), axis=1), ones)
        ssq = both[:, :n] + both[:, n:]
    else:
        ssq = jnp.concatenate([_dot_exact_lhs(sq[:, i:i + w], ones, 2) for i in range(0, n, w)],
                              axis=1)
    return x * lax.rsqrt(ssq * (1.0 / HEAD_DIM) + EPS) * g_ref[...]


def _in_proj(h, win_ref, lo, hi):
    return _dot(h, win_ref[:, lo:hi])


def _expand_head_lanes(v):
    n = v.shape[0]
    lo = _lane_lo((n, 128))
    pairs = []
    for j in range(SSD_HEADS // 2):
        even = jnp.broadcast_to(v[:, 2 * j:2 * j + 1], (n, 128))
        odd = jnp.broadcast_to(v[:, 2 * j + 1:2 * j + 2], (n, 128))
        pairs.append(jnp.where(lo, even, odd))
    return jnp.concatenate(pairs, axis=1)


def _decay_terms(h, win_ref, dtb_ref, alog_ref, tril_ref):
    dt = _softplus(_in_proj(h, win_ref, C_DT, C_Q) + dtb_ref[...])
    acs = _dot_exact_rhs(tril_ref[...], dt * (-jnp.exp(alog_ref[...])), 3)
    return _expand_head_lanes(dt), _expand_head_lanes(acs)


def _run_ordered(order, gens):
    for who in order:
        next(gens[who], None)
    for g in gens.values():
        _drain(g)


def _ada_kernel(c_ref, w_ref, b_ref, o_ref):
    c = c_ref[...]
    o_ref[0] = _dot(_silu(c).astype(BF16), w_ref[0].astype(BF16)) + b_ref[0]


def _ada_call(c_all, w_ada, b_ada):
    depth = w_ada.shape[0]
    n = c_all.shape[0]
    blk = 2048
    return pl.pallas_call(
        _ada_kernel,
        grid=(depth, 6 * D_MODEL // blk),
        in_specs=[
            pl.BlockSpec((n, D_MODEL), lambda l, j: (0, 0)),
            pl.BlockSpec((1, D_MODEL, blk), lambda l, j: (l, 0, j)),
            pl.BlockSpec((1, 1, blk), lambda l, j: (l, 0, j)),
        ],
        out_specs=pl.BlockSpec((1, n, blk), lambda l, j: (l, 0, j)),
        out_shape=jax.ShapeDtypeStruct((depth, n, 6 * D_MODEL), F32),
        compiler_params=pltpu.CompilerParams(
            dimension_semantics=("arbitrary", "arbitrary"), vmem_limit_bytes=VMEM_LIMIT),
        name="adaln",
    )(c_all, w_ada, b_ada.reshape(depth, 1, 6 * D_MODEL))


def _mod_norm(x, sh, sc, g_ref):
    ms = jnp.mean(x * x, axis=-1, keepdims=True)
    return (x * lax.rsqrt(ms + EPS) * (g_ref[...] * (1.0 + sc)) + sh).astype(BF16)


def _ffn_norm(x, sh, sc, nffn_ref):
    return _mod_norm(x, sh, sc, nffn_ref)


def _ffn_front_steps(put_act, h, hist_ref, wup_ref, cw_ref, cb_ref, tail_ref, ubuf, gbuf, n_seq,
                     length):
    rows = n_seq * length

    def up(j):
        return [_dot(h, wup_ref[:, base + j * FF_BLK:base + (j + 1) * FF_BLK]) for base in (0, D_FF)]

    raws = up(0)
    yield
    for j in range(N_FF_BLK):
        nxt = up(j + 1) if j + 1 < N_FF_BLK else None
        halves = []
        for raw, buf, base in zip(raws, (ubuf, gbuf), (0, D_FF)):
            lo = base + j * FF_BLK
            cw, cb = cw_ref.at[:, lo:lo + FF_BLK], cb_ref.at[:, lo:lo + FF_BLK]
            if n_seq == 1:
                y = _causal_conv_rolled(raw, hist_ref[0, :, lo:lo + FF_BLK], cw, cb, FFN_CONV)
                tail_ref[0, :, lo:lo + FF_BLK] = raw[length - HIST:length]
            else:
                buf[:, 0:HIST, :] = hist_ref[:, :, lo:lo + FF_BLK]
                y = _causal_conv(buf, raw.reshape(n_seq, length, FF_BLK), cw, cb, FFN_CONV)
                tail_ref[:, :, lo:lo + FF_BLK] = buf[:, length:length + HIST, :]
            halves.append(y.reshape(rows, FF_BLK))
        put_act(j, (_silu(halves[1]) * halves[0]).astype(BF16))
        raws = nxt
        yield


FFN_K_SPLIT = (N_FF_BLK // 2 + 1) * FF_BLK


def _split_acts(acts):
    split = FFN_K_SPLIT // FF_BLK
    return (jnp.concatenate([acts[j] for j in range(split)], axis=1),
            jnp.concatenate([acts[j] for j in range(split, N_FF_BLK)], axis=1))


def _ffn_back_steps(put_out, x, g, act_a, act_b, wdn_ref):
    for n in range(D_MODEL // FF_BLK):
        cols = slice(n * FF_BLK, (n + 1) * FF_BLK)
        d = _dot(act_a, wdn_ref[0:FFN_K_SPLIT, cols])
        yield
        d = d + _dot(act_b, wdn_ref[FFN_K_SPLIT:D_FF, cols])
        put_out(cols, x[:, cols] + g[:, cols] * d)
        yield


def _mixer_prompt_steps(res, x, mod, t, nmix_ref, win_ref, cw_ref, cb_ref, dtb_ref, alog_ref,
                        dsk_ref, ssdn_ref, qn_ref, kn_ref, sink_ref, bias_ref, tril_ref, ones_ref,
                        smask_ref, wout_ref, xbc_buf, kbuf, vbuf, s2_ref, tile):
    n_chunks = tile // CHUNK
    sh1, sc1, g1 = mod[0:1], mod[1:2], mod[2:3]
    h = _mod_norm(x, sh1, sc1, nmix_ref)

    xbc_raw = _in_proj(h, win_ref, C_XBC, C_DT)
    yield
    conv = _causal_conv_rolled(xbc_raw, xbc_buf[...], cw_ref, cb_ref, SSD_CONV)
    res["tail"] = xbc_raw[tile - HIST:tile]
    xbc_buf[...] = res["tail"]
    xbc = _silu(conv)
    xs, bm, cm = xbc[:, 0:D_SSD], xbc[:, D_SSD:D_SSD + D_BC], xbc[:, D_SSD + D_BC:D_XBC]
    yield
    dt, acs = _decay_terms(h, win_ref, dtb_ref, alog_ref, tril_ref)
    yield
    qn = _head_rms(_in_proj(h, win_ref, C_Q, C_K), ones_ref, qn_ref)
    kv = _in_proj(h, win_ref, C_K, C_END)
    kn = _head_rms(kv[:, 0:D_KV], ones_ref, kn_ref)
    vn = kv[:, D_KV:2 * D_KV]
    z = _in_proj(h, win_ref, C_Z, C_XBC)
    yield
    smask = smask_ref[...]
    dskip = dsk_ref[...]
    band = WINDOW + CHUNK
    kds = _dup_halves(jnp.concatenate([kbuf[...], kn], axis=0))
    vds = _dup_halves(jnp.concatenate([vbuf[...], vn], axis=0))
    res["k_last"] = kn[tile - WINDOW:tile]
    res["v_last"] = vn[tile - WINDOW:tile]
    kbuf[...] = res["k_last"]
    vbuf[...] = res["v_last"]
    state = {"s2": s2_ref[...]}
    part = {}
    items = []
    for c in range(n_chunks):
        r = slice(c * CHUNK, (c + 1) * CHUNK)
        kr = slice(c * CHUNK, c * CHUNK + band)
        first_valid = (WINDOW // CHUNK - (t * n_chunks + c)) * CHUNK
        items.append(_ssd_steps(part, ("y", c), acs[r], xs[r], dt[r], bm[r], cm[r], state, dskip,
                                smask, CHUNK))
        for g in range(2):
            items.append(_attn_steps(part, ("o", c, g), qn[r, 256 * g:256 * (g + 1)], kds[g][kr],
                                     vds[g][kr], bias_ref[g], sink_ref[g], first_valid, CHUNK))
    yield from _run_staggered(items)
    s2_ref[...] = state["s2"]
    y_ssd = _rms(jnp.concatenate([part["y", c] for c in range(n_chunks)], axis=0) * _silu(z),
                 ssdn_ref[...])
    o = jnp.concatenate([jnp.concatenate([part["o", c, 0], part["o", c, 1]], axis=1)
                         for c in range(n_chunks)], axis=0)
    yield
    mix = (_dot(y_ssd.astype(BF16), wout_ref[0:D_SSD, :])
           + _dot(o.astype(BF16), wout_ref[D_SSD:D_SSD + D_ATTN, :]))
    res["x_mid"] = x + g1 * mix


def _layer_prompt_kernel(x_ref, modm_ref, modf_ref,
                         nmix_ref, win_ref, cw_ref, cb_ref, dtb_ref, alog_ref, dsk_ref, ssdn_ref,
                         qn_ref, kn_ref, sink_ref, bias_ref, tril_ref, ones_ref,
                         smask_ref, wout_ref,
                         nffn_ref, wup_ref, fcw_ref, fcb_ref, wdn_ref,
                         xo_ref, ssm_ref, ctail_ref, ko_ref, vo_ref, ftail_ref,
                         xbc_buf, kbuf, vbuf, s2_ref, xmid_buf, hffn_buf, hist, *, tile, n_tiles):
    s = pl.program_id(0)
    t_mix = lax.rem(s, n_tiles)
    t_ffn = lax.rem(s + n_tiles - 1, n_tiles)
    slot_w = lax.rem(s, 2)
    slot_r = 1 - slot_w

    @pl.when(s == 0)
    def _():
        xmid_buf[...] = jnp.zeros(xmid_buf.shape, F32)
        hffn_buf[...] = jnp.zeros(hffn_buf.shape, BF16)

    @pl.when(t_mix == 0)
    def _():
        xbc_buf[...] = jnp.zeros((HIST, D_XBC), F32)
        kbuf[...] = jnp.zeros((WINDOW, D_KV), F32)
        vbuf[...] = jnp.zeros((WINDOW, D_KV), F32)
        s2_ref[...] = jnp.zeros((2 * D_STATE, D_SSD), F32)

    @pl.when(jnp.logical_or(t_ffn == 0, s == 0))
    def _():
        hist[...] = jnp.zeros(hist.shape, F32)

    mres = {}
    modm = modm_ref[0]
    mixer = _mixer_prompt_steps(
        mres, x_ref[0], modm, t_mix, nmix_ref, win_ref, cw_ref, cb_ref, dtb_ref, alog_ref,
        dsk_ref, ssdn_ref, qn_ref, kn_ref, sink_ref, bias_ref, tril_ref, ones_ref,
        smask_ref, wout_ref, xbc_buf, kbuf, vbuf, s2_ref, tile)

    def mixer_then_norm():
        yield from mixer
        yield
        xmid_buf[slot_w] = mres["x_mid"]
        hffn_buf[slot_w] = _ffn_norm(mres["x_mid"], modm[3:4], modm[4:5], nffn_ref)

    def put_out(cols, v):
        xo_ref[0, :, cols] = v

    def ffn():
        acts = {}
        yield from _ffn_front_steps(acts.__setitem__, hffn_buf[slot_r], hist, wup_ref, fcw_ref,
                                    fcb_ref, hist, None, None, 1, tile)
        yield from _ffn_back_steps(put_out, xmid_buf[slot_r], modf_ref[0, 5:6, :],
                                   *_split_acts(acts), wdn_ref)

    _run_ordered(STEP_ORDER, {"f": ffn(), "m": mixer_then_norm()})

    @pl.when(t_mix == n_tiles - 1)
    def _():
        ssm_ref[0] = _state_rows(s2_ref[...])
        ctail_ref[0] = mres["tail"]
        ko_ref[0] = mres["k_last"]
        vo_ref[0] = mres["v_last"]

    @pl.when(jnp.logical_and(t_ffn == n_tiles - 1, s > 0))
    def _():
        ftail_ref[...] = hist[...]


def _const_spec(shape):
    zeros = (0,) * len(shape)
    return pl.BlockSpec(shape, lambda *_: zeros, pipeline_mode=pl.Buffered(1))


def _layer_spec(a, l):
    zeros = (0,) * (a.ndim - 1)
    return pl.BlockSpec((None,) + a.shape[1:], lambda *_: (l,) + zeros,
                        pipeline_mode=pl.Buffered(1))


def _layer_prompt_call(x, mod, lw, consts, l, tile):
    b, length, _ = x.shape
    n_tiles = length // tile
    n_steps = b * n_tiles
    band = WINDOW + CHUNK
    kern = functools.partial(_layer_prompt_kernel, tile=tile, n_tiles=n_tiles)

    def mix_tile(s):
        return jnp.minimum(s, n_steps - 1)

    def ffn_tile(s):
        return jnp.maximum(s - 1, 0)

    lay = lambda a: (a, _layer_spec(a, l))
    con = lambda a: (a, _const_spec(a.shape))
    params = [lay(lw["norm_mix"]), lay(lw["w_in"]), lay(lw["conv_w"]), lay(lw["conv_b"]),
              lay(lw["dt_bias"]), lay(lw["a_log"]), lay(lw["d_skip"]), lay(lw["ssd_norm"]),
              lay(lw["q_norm"]), lay(lw["k_norm"]), lay(lw["sink_p"]), con(consts["bias_p"]),
              con(consts["tril_p"]), con(consts["ones"]),
              con(consts["smask"]), lay(lw["w_out"]),
              lay(lw["norm_ffn"]), lay(lw["w_up"]), lay(lw["conv_ffn_w"]), lay(lw["conv_ffn_b"]),
              lay(lw["w_down"])]
    assert consts["bias_p"].shape == (2, band, 4 * CHUNK)
    return pl.pallas_call(
        kern,
        grid=(n_steps + 1,),
        in_specs=[
            pl.BlockSpec((1, tile, D_MODEL),
                         lambda s: (mix_tile(s) // n_tiles, mix_tile(s) % n_tiles, 0)),
            pl.BlockSpec((None, 1, 8, D_MODEL), lambda s: (l, mix_tile(s) // n_tiles, 0, 0)),
            pl.BlockSpec((None, 1, 8, D_MODEL), lambda s: (l, ffn_tile(s) // n_tiles, 0, 0)),
        ] + [spec for _, spec in params],
        out_specs=[
            pl.BlockSpec((1, tile, D_MODEL),
                         lambda s: (ffn_tile(s) // n_tiles, ffn_tile(s) % n_tiles, 0)),
            pl.BlockSpec((1, D_SSD, D_STATE), lambda s: (mix_tile(s) // n_tiles, 0, 0)),
            pl.BlockSpec((1, HIST, D_XBC), lambda s: (mix_tile(s) // n_tiles, 0, 0)),
            pl.BlockSpec((1, WINDOW, D_KV), lambda s: (mix_tile(s) // n_tiles, 0, 0)),
            pl.BlockSpec((1, WINDOW, D_KV), lambda s: (mix_tile(s) // n_tiles, 0, 0)),
            pl.BlockSpec((1, HIST, 2 * D_FF), lambda s: (ffn_tile(s) // n_tiles, 0, 0)),
        ],
        out_shape=[
            jax.ShapeDtypeStruct((b, length, D_MODEL), F32),
            jax.ShapeDtypeStruct((b, D_SSD, D_STATE), F32),
            jax.ShapeDtypeStruct((b, HIST, D_XBC), F32),
            jax.ShapeDtypeStruct((b, WINDOW, D_KV), F32),
            jax.ShapeDtypeStruct((b, WINDOW, D_KV), F32),
            jax.ShapeDtypeStruct((b, HIST, 2 * D_FF), F32),
        ],
        scratch_shapes=[
            pltpu.VMEM((HIST, D_XBC), F32),
            pltpu.VMEM((WINDOW, D_KV), F32),
            pltpu.VMEM((WINDOW, D_KV), F32),
            pltpu.VMEM((2 * D_STATE, D_SSD), F32),
            pltpu.VMEM((2, tile, D_MODEL), F32),
            pltpu.VMEM((2, tile, D_MODEL), BF16),
            pltpu.VMEM((1, HIST, 2 * D_FF), F32),
        ],
        compiler_params=pltpu.CompilerParams(
            dimension_semantics=("arbitrary",), vmem_limit_bytes=VMEM_LIMIT),
        name="layer_prompt",
    )(x, mod, mod, *[a for a, _ in params])


def _seq_rows(mod_ref, k, length):
    v = mod_ref[k]
    n_seq, d = v.shape
    return jnp.broadcast_to(v[:, None, :], (n_seq, length, d)).reshape(n_seq * length, d)


def _mixer_sample_kernel(x_ref, mod_ref, ssm_in_ref, ccache_ref, kc_ref, vc_ref,
                         nmix_ref, win_ref, cw_ref, cb_ref, dtb_ref, alog_ref,
                         dsk_ref, ssdn_ref, qn_ref, kn_ref, sink_ref, bias_ref, tril_ref,
                         ones_ref, smask_ref, wout_ref,
                         xo_ref, ssm_ref, ctail_ref, ko_ref, vo_ref,
                         xbc_buf, *, n_seq, length):
    rows = n_seq * length
    x = x_ref[...]
    h = (_rms(x, nmix_ref[...]) * (1.0 + _seq_rows(mod_ref, 1, length))
         + _seq_rows(mod_ref, 0, length)).astype(BF16)

    xbc_raw = _in_proj(h, win_ref, C_XBC, C_DT)
    xbc_buf[:, 0:HIST, :] = ccache_ref[...]
    conv = _causal_conv(xbc_buf, xbc_raw.reshape(n_seq, length, D_XBC), cw_ref, cb_ref, SSD_CONV)
    ctail_ref[...] = xbc_buf[:, length:length + HIST, :]
    xbc = _silu(conv.reshape(rows, D_XBC))
    xs, bm, cm = xbc[:, 0:D_SSD], xbc[:, D_SSD:D_SSD + D_BC], xbc[:, D_SSD + D_BC:D_XBC]
    dt, acs = _decay_terms(h, win_ref, dtb_ref, alog_ref, tril_ref)
    qn = _head_rms(_in_proj(h, win_ref, C_Q, C_K), ones_ref, qn_ref)
    kv = _in_proj(h, win_ref, C_K, C_END)
    kn = _head_rms(kv[:, 0:D_KV], ones_ref, kn_ref)
    vn = kv[:, D_KV:2 * D_KV]
    smask = smask_ref[...]
    dskip = dsk_ref[...]
    part = {}

    def ssd_item(i, r):
        state = {"s2": ssm_in_ref[i]}
        yield from _ssd_steps(part, ("y", i), acs[r], xs[r], dt[r], bm[r], cm[r], state, dskip,
                              smask, length)
        ssm_ref[i] = _state_rows(state["s2"])

    def attn_item(i, r):
        kband = jnp.concatenate([kc_ref[i], kn[r]], axis=0)
        vband = jnp.concatenate([vc_ref[i], vn[r]], axis=0)
        ko_ref[i] = kband[length:length + WINDOW]
        vo_ref[i] = vband[length:length + WINDOW]
        kds, vds = _dup_halves(kband), _dup_halves(vband)
        groups = [_attn_steps(part, ("o", i, g), qn[r, 256 * g:256 * (g + 1)], kds[g], vds[g],
                              bias_ref[g], sink_ref[g], None, length) for g in range(2)]
        yield from _run_staggered(groups, stagger=False)

    items = []
    for i in range(n_seq):
        r = slice(i * length, (i + 1) * length)
        items += [ssd_item(i, r), attn_item(i, r)]
    _drain(_run_staggered(items))
    y = jnp.concatenate([part["y", i] for i in range(n_seq)], axis=0)
    o = jnp.concatenate([jnp.concatenate([part["o", i, 0], part["o", i, 1]], axis=1)
                         for i in range(n_seq)], axis=0)

    z = _in_proj(h, win_ref, C_Z, C_XBC)
    y_ssd = _rms(y * _silu(z), ssdn_ref[...])
    mix = (_dot(y_ssd.astype(BF16), wout_ref[0:D_SSD, :])
           + _dot(o.astype(BF16), wout_ref[D_SSD:D_SSD + D_ATTN, :]))
    xo_ref[...] = x + _seq_rows(mod_ref, 2, length) * mix


def _lead_spec(shape):
    tail_zeros = (0,) * (len(shape) - 1)
    return pl.BlockSpec(shape, lambda i: (i,) + tail_zeros)


def _mod_spec(n_blk, l):
    return pl.BlockSpec((None, 6, n_blk, D_MODEL), lambda i: (l, 0, i, 0))


def _seq_spec(a, n_blk, l):
    zeros = (0,) * (a.ndim - 2)
    return pl.BlockSpec((None, n_blk) + a.shape[2:], lambda i: (l, i) + zeros)


def _mixer_sample_call(x2, mod, ssm_in, ccache, kc, vc, lw, consts, l, n_blk, length):
    n_seq = ssm_in.shape[1]
    rows = n_blk * length
    kern = functools.partial(_mixer_sample_kernel, n_seq=n_blk, length=length)
    seq_args = (ssm_in, ccache, kc, vc)
    lay = lambda a: (a, _layer_spec(a, l))
    con = lambda a: (a, _const_spec(a.shape))
    params = [lay(lw["norm_mix"]), lay(lw["w_in"]), lay(lw["conv_w"]), lay(lw["conv_b"]),
              lay(lw["dt_bias"]), lay(lw["a_log"]), lay(lw["d_skip"]), lay(lw["ssd_norm"]),
              lay(lw["q_norm"]), lay(lw["k_norm"]), lay(lw["sink_s"]), con(consts["bias_s"]),
              con(consts["tril_s"]), con(consts["ones"]),
              con(consts["smask"]), lay(lw["w_out"])]
    out_shape = [
        jax.ShapeDtypeStruct((n_seq * length, D_MODEL), F32),
        jax.ShapeDtypeStruct((n_seq, D_SSD, D_STATE), F32),
        jax.ShapeDtypeStruct((n_seq, HIST, D_XBC), F32),
        jax.ShapeDtypeStruct((n_seq, WINDOW, D_KV), F32),
        jax.ShapeDtypeStruct((n_seq, WINDOW, D_KV), F32),
    ]
    return pl.pallas_call(
        kern,
        grid=(n_seq // n_blk,),
        in_specs=([_lead_spec((rows, D_MODEL)), _mod_spec(n_blk, l)]
                  + [_seq_spec(a, n_blk, l) for a in seq_args]
                  + [spec for _, spec in params]),
        out_specs=([_lead_spec((rows, D_MODEL))]
                   + [_lead_spec((n_blk,) + s.shape[1:]) for s in out_shape[1:]]),
        out_shape=out_shape,
        scratch_shapes=[
            pltpu.VMEM((n_blk, HIST + length, D_XBC), F32),
        ],
        compiler_params=pltpu.CompilerParams(
            dimension_semantics=("arbitrary",), vmem_limit_bytes=VMEM_LIMIT),
        name="mixer_sample",
    )(x2, mod, *seq_args, *[a for a, _ in params])


def _ffn_sample_kernel(x_ref, mod_ref, hist_ref, nffn_ref, wup_ref, cw_ref, cb_ref,
                       wdn_ref, xo_ref, tail_ref, ubuf, gbuf, *, n_seq, length):
    x = x_ref[...]
    h = _ffn_norm(x, _seq_rows(mod_ref, 3, length), _seq_rows(mod_ref, 4, length), nffn_ref)
    acts = {}

    def put_out(cols, v):
        xo_ref[:, cols] = v

    _drain(_ffn_front_steps(acts.__setitem__, h, hist_ref, wup_ref, cw_ref, cb_ref, tail_ref, ubuf,
                            gbuf, n_seq, length))
    _drain(_ffn_back_steps(put_out, x, _seq_rows(mod_ref, 5, length), *_split_acts(acts),
                           wdn_ref))


def _ffn_sample_call(x2, mod, hist, lw, l, n_blk, length):
    n_seq = hist.shape[1]
    rows = n_blk * length
    kern = functools.partial(_ffn_sample_kernel, n_seq=n_blk, length=length)
    const_args = (lw["norm_ffn"], lw["w_up"], lw["conv_ffn_w"], lw["conv_ffn_b"], lw["w_down"])
    out_shape = [
        jax.ShapeDtypeStruct((n_seq * length, D_MODEL), F32),
        jax.ShapeDtypeStruct((n_seq, HIST, 2 * D_FF), F32),
    ]
    return pl.pallas_call(
        kern,
        grid=(n_seq // n_blk,),
        in_specs=([_lead_spec((rows, D_MODEL)), _mod_spec(n_blk, l), _seq_spec(hist, n_blk, l)]
                  + [_layer_spec(a, l) for a in const_args]),
        out_specs=[_lead_spec((rows, D_MODEL)), _lead_spec((n_blk, HIST, 2 * D_FF))],
        out_shape=out_shape,
        scratch_shapes=[
            pltpu.VMEM((n_blk, HIST + length, FF_BLK), F32),
            pltpu.VMEM((n_blk, HIST + length, FF_BLK), F32),
        ],
        compiler_params=pltpu.CompilerParams(
            dimension_semantics=("arbitrary",), vmem_limit_bytes=VMEM_LIMIT),
        name="ffn_sample",
    )(x2, mod, hist, *const_args)


def _expand_heads(v):
    return jnp.repeat(v.astype(F32), SSD_HEAD_DIM, axis=1)[:, None, :]


def _pad_heads(v):
    return jnp.pad(v.astype(F32), ((0, 0), (0, C_Q - C_DT - SSD_HEADS)))[:, None, :]


def _prep_params(w_in, conv_ssd_w, conv_ssd_b, dt_bias, a_log, d_skip, ssd_norm, q_norm, k_norm,
                 sinks, w_out, norm_mix, norm_ffn, w_up, conv_ffn_w, conv_ffn_b, w_down, q_sample):
    o_dt = D_SSD + D_XBC
    o_q = o_dt + SSD_HEADS
    w_dt = jnp.pad(w_in[:, :, o_dt:o_q], ((0, 0), (0, 0), (0, C_Q - C_DT - SSD_HEADS)))
    w_in_r = jnp.concatenate(
        [w_in[:, :, :o_dt], w_dt, w_in[:, :, o_q:]], axis=2).astype(BF16)
    assert w_in_r.shape[2] == C_END
    return {
        "norm_mix": norm_mix[:, None, :],
        "w_in": w_in_r,
        "conv_w": conv_ssd_w,
        "conv_b": conv_ssd_b[:, None, :],
        "dt_bias": _pad_heads(dt_bias),
        "a_log": _pad_heads(a_log),
        "d_skip": _expand_heads(d_skip),
        "ssd_norm": ssd_norm[:, None, :],
        "q_norm": jnp.tile(q_norm * HEAD_DIM ** -0.5, (1, ATTN_HEADS))[:, None, :],
        "k_norm": jnp.tile(k_norm, (1, D_KV // HEAD_DIM))[:, None, :],
        "sink_p": _sink_rows(sinks, CHUNK),
        "sink_s": _sink_rows(sinks, q_sample),
        "w_out": w_out.astype(BF16),
        "norm_ffn": norm_ffn[:, None, :],
        "w_up": w_up.astype(BF16),
        "conv_ffn_w": conv_ffn_w,
        "conv_ffn_b": conv_ffn_b[:, None, :],
        "w_down": w_down.astype(BF16),
    }


def _block_tril(n, blk):
    i = np.arange(n)
    return jnp.asarray(((i[:, None] // blk) == (i[None, :] // blk)) & (i[None, :] <= i[:, None]),
                       dtype=BF16)


def _block_ones(n, blk):
    i = np.arange(n)
    return jnp.asarray((i[:, None] // blk) == (i[None, :] // blk), dtype=BF16)


def _alibi_bias(q):
    slopes = (2.0 ** (-8.0 * np.arange(1, ATTN_HEADS + 1) / ATTN_HEADS)).astype(np.float32)
    dist = np.abs(np.arange(q)[:, None] + WINDOW - np.arange(WINDOW + q)[None, :]).astype(np.float32)
    bias = slopes[:, None, None] * dist[None]
    return jnp.asarray(np.transpose(bias.reshape(2, 4 * q, WINDOW + q), (0, 2, 1)))


def _sink_rows(sinks, q):
    return jnp.repeat(sinks.astype(F32), q, axis=1).reshape(sinks.shape[0], 2, 1, 4 * q)


def _state_mask():
    r = np.arange(2 * D_STATE)[:, None] // D_STATE
    c = np.arange(D_SSD)[None, :] // (D_SSD // 2)
    return jnp.asarray(r == c, dtype=F32)


def _state_to_s2(state):
    lead = state.shape[:2]
    st = jnp.transpose(state, (0, 1, 4, 2, 3)).reshape(lead + (D_STATE, D_SSD))
    half = D_SSD // 2
    zero = jnp.zeros(lead + (D_STATE, half), state.dtype)
    top = jnp.concatenate([st[..., :half], zero], axis=-1)
    bot = jnp.concatenate([zero, st[..., half:]], axis=-1)
    return jnp.concatenate([top, bot], axis=-2)


def _pad_hist(cache):
    k = cache.shape[2]
    return jnp.pad(cache, ((0, 0), (0, 0), (HIST - k, 0), (0, 0)))


def kernel(x_prompt, x_sample, c_prompt, c_sample, state_ssm, cache_conv_ssd, cache_attn_k,
           cache_attn_v, cache_conv_ffn, w_ada, b_ada, norm_mix, w_in, conv_ssd_w, conv_ssd_b,
           dt_bias, a_log, d_skip, ssd_norm, q_norm, k_norm, sinks, w_out, norm_ffn, w_up,
           conv_ffn_w, conv_ffn_b, w_down):
    depth = w_ada.shape[0]
    bp, seq, _ = x_prompt.shape
    bs, lsamp, _ = x_sample.shape
    rows_s = bs * lsamp
    tile = TILE_T
    n_blk = SAMPLE_SEQS

    mod_all = _ada_call(jnp.concatenate([c_prompt, c_sample], axis=0), w_ada, b_ada)
    mod_all = mod_all.reshape(depth, bp + bs, 6, D_MODEL)
    mod_p = jnp.pad(mod_all[:, :bp], ((0, 0), (0, 0), (0, 2), (0, 0)))
    mod_s = jnp.transpose(mod_all[:, bp:], (0, 2, 1, 3))

    consts = {
        "tril_p": _block_tril(tile, CHUNK),
        "tril_s": _block_tril(n_blk * lsamp, lsamp),
        "ones": _block_ones(MXU_TILE, HEAD_DIM),
        "smask": _state_mask(),
        "bias_p": _alibi_bias(CHUNK),
        "bias_s": _alibi_bias(lsamp),
    }

    lw = _prep_params(w_in, conv_ssd_w, conv_ssd_b, dt_bias, a_log, d_skip, ssd_norm, q_norm, k_norm,
                      sinks, w_out, norm_mix, norm_ffn, w_up, conv_ffn_w, conv_ffn_b, w_down, lsamp)
    ssm_in = _state_to_s2(state_ssm)
    conv_in = _pad_hist(cache_conv_ssd)
    ffn_in = _pad_hist(cache_conv_ffn)
    k_in = cache_attn_k.reshape(depth, bs, WINDOW, D_KV)
    v_in = cache_attn_v.reshape(depth, bs, WINDOW, D_KV)

    xp = x_prompt
    xs = x_sample.reshape(rows_s, D_MODEL)
    outs = [[] for _ in range(10)]
    for l in range(depth):
        xp, ssm_p, ct_p, k_p, v_p, ft_p = _layer_prompt_call(xp, mod_p, lw, consts, l, tile)
        xs, ssm_s, ct_s, k_s, v_s = _mixer_sample_call(
            xs, mod_s, ssm_in, conv_in, k_in, v_in, lw, consts, l, n_blk, lsamp)
        xs, ft_s = _ffn_sample_call(xs, mod_s, ffn_in, lw, l, n_blk, lsamp)

        kv_shape = (WINDOW, D_KV // HEAD_DIM, HEAD_DIM)
        for lst, val in zip(outs, (
                ssm_p.reshape(bp, SSD_HEADS, SSD_HEAD_DIM, D_STATE),
                ssm_s.reshape(bs, SSD_HEADS, SSD_HEAD_DIM, D_STATE),
                ct_p[:, HIST - (SSD_CONV - 1):], ct_s[:, HIST - (SSD_CONV - 1):],
                k_p.reshape((bp,) + kv_shape), k_s.reshape((bs,) + kv_shape),
                v_p.reshape((bp,) + kv_shape), v_s.reshape((bs,) + kv_shape),
                ft_p[:, HIST - (FFN_CONV - 1):], ft_s[:, HIST - (FFN_CONV - 1):])):
            lst.append(val)

    return (xp, xs.reshape(bs, lsamp, D_MODEL)) + tuple(jnp.stack(o) for o in outs)
```

```python
import functools

import numpy as np
import jax
import jax.numpy as jnp
from jax import lax
from jax.experimental import pallas as pl
from jax.experimental.pallas import tpu as pltpu

F32 = jnp.float32
BF16 = jnp.bfloat16

D_MODEL = 1024
SSD_HEADS = 8
SSD_HEAD_DIM = 64
D_SSD = 512
D_STATE = 64
D_BC = 128
SSD_CONV = 4
D_XBC = 768
ATTN_HEADS = 8
HEAD_DIM = 64
D_ATTN = 512
D_KV = 128
WINDOW = 128
CHUNK = 64
D_FF = 2816
FFN_CONV = 3
EPS = 1e-6

C_Z, C_XBC, C_DT, C_Q, C_K, C_V, C_END = 0, 512, 1280, 1408, 1920, 2048, 2176

MXU_TILE = 256
HIST = 8
FF_BLK = 256
N_FF_BLK = D_FF // FF_BLK
STEP_ORDER = "m" * 4 + "f" * 12 + "fm" * 8
TILE_T = 256
SAMPLE_SEQS = 16
VMEM_LIMIT = 56 * 1024 * 1024


def _dot(a, b):
    return jnp.dot(a, b, preferred_element_type=F32)


def _dot_nt(a, b):
    return lax.dot_general(a, b, (((1,), (1,)), ((), ())), preferred_element_type=F32)


def _dot_tn(a, b):
    return lax.dot_general(a, b, (((0,), (0,)), ((), ())), preferred_element_type=F32)


def _split_bf16(x, parts):
    out = []
    r = x
    for i in range(parts):
        p = r.astype(BF16)
        out.append(p)
        if i + 1 < parts:
            r = r - p.astype(F32)
    return out


def _dot_exact_rhs(a_bf16, x, parts):
    acc = None
    for p in _split_bf16(x, parts):
        d = _dot(a_bf16, p)
        acc = d if acc is None else acc + d
    return acc


def _dot_exact_lhs(x, b_bf16, parts):
    acc = None
    for p in _split_bf16(x, parts):
        d = _dot(p, b_bf16)
        acc = d if acc is None else acc + d
    return acc


def _rms(x, g):
    ms = jnp.mean(x * x, axis=-1, keepdims=True)
    return x * lax.rsqrt(ms + EPS) * g


def _silu(x):
    return x * (1.0 / (1.0 + jnp.exp2(x * (-np.log2(np.e)))))


def _softplus(x):
    return jnp.maximum(x, 0.0) + jnp.log1p(jnp.exp(-jnp.abs(x)))


def _lane_lo(shape):
    return (lax.broadcasted_iota(jnp.int32, shape, len(shape) - 1) % 128) < 64


def _causal_conv(buf_ref, raw3, w_ref, b_ref, taps):
    length = raw3.shape[1]
    buf_ref[:, HIST:HIST + length, :] = raw3
    y = b_ref[...]
    for k in range(taps):
        lo = HIST - (taps - 1) + k
        y = y + buf_ref[:, lo:lo + length, :] * w_ref[k:k + 1, :]
    return y


def _causal_conv_rolled(raw, hist, w_ref, b_ref, taps):
    row = lax.broadcasted_iota(jnp.int32, hist.shape, 0)
    y = b_ref[...]
    for k in range(taps):
        s = taps - 1 - k
        if s == 0:
            term = raw
        else:
            rolled = pltpu.roll(raw, s, axis=0)
            head = jnp.where(row < s, pltpu.roll(hist, s, axis=0), rolled[0:HIST])
            term = jnp.concatenate([head, rolled[HIST:]], axis=0)
        y = y + term * w_ref[k:k + 1, :]
    return y


def _ssd_steps(res, key, acs, xs, dt, bm, cm, state, dskip, smask, q):
    pad = CHUNK - q

    def stack2(a, b):
        if pad:
            zero = jnp.zeros((pad, a.shape[1]), a.dtype)
            return jnp.concatenate([a, zero, b, zero], axis=0)
        return jnp.concatenate([a, b], axis=0)

    lane = lax.broadcasted_iota(jnp.int32, (q, D_SSD), 1)
    row = lax.broadcasted_iota(jnp.int32, (q, D_SSD), 0)
    rowf = jnp.sum(jnp.where((lane % CHUNK) == row, acs, 0.0), axis=0, keepdims=True)
    a_end = acs[q - 1:q, :]
    xdt = xs * dt
    xw = (xdt * jnp.exp(a_end - acs)).astype(BF16)
    cm_b = cm.astype(BF16)
    zrow = jnp.zeros((2 * D_STATE - q, D_BC), F32)
    bm_t = jnp.concatenate([bm, zrow], axis=0).T.astype(BF16)
    xw_pad = jnp.concatenate([xw, jnp.zeros((2 * D_STATE - q, D_SSD), BF16)], axis=0)
    lo128 = _lane_lo((q, 128))
    bstack = stack2(bm, bm).astype(BF16)
    yield
    s_add = _dot(bm_t, xw_pad) * smask
    cbs = [_dot_nt(jnp.where(lo128 if g == 0 else ~lo128, cm, 0.0).astype(BF16), bstack)
           for g in range(2)]
    yield
    lane128 = lax.broadcasted_iota(jnp.int32, (q, 128), 1)
    row128 = lax.broadcasted_iota(jnp.int32, (q, 128), 0)
    causal = (lane128 % CHUNK) <= row128
    ms, rhss = [], []
    for j in range(4):
        seg = acs[:, 128 * j:128 * (j + 1)] - rowf[:, 128 * j:128 * (j + 1)]
        decay = jnp.exp(jnp.where(causal, seg, -jnp.inf))
        ms.append((cbs[j // 2] * decay).astype(BF16))
        xp = xdt[:, 128 * j:128 * (j + 1)]
        rhss.append(stack2(jnp.where(lo128, xp, 0.0), jnp.where(lo128, 0.0, xp)).astype(BF16))
    yield
    s2 = state["s2"]
    y_off = _dot(cm_b, s2.astype(BF16)) * jnp.exp(acs)
    state["s2"] = s2 * jnp.exp(a_end) + s_add
    y_diag = jnp.concatenate([_dot(m, r) for m, r in zip(ms, rhss)], axis=1)
    res[key] = y_diag + y_off + dskip * xs


def _dup_halves(x):
    lo = _lane_lo(x.shape)
    sw = pltpu.roll(x, 64, axis=1)
    return jnp.where(lo, x, sw).astype(BF16), jnp.where(lo, sw, x).astype(BF16)


def _attn_steps(res, key, qg, kd, vd, bias, sink, first_valid, q):
    band = kd.shape[0]
    lo_q = _lane_lo((q, 128))
    p0, p1 = qg[:, 0:128], qg[:, 128:256]
    qs = jnp.concatenate([jnp.where(lo_q, p0, 0.0), jnp.where(lo_q, 0.0, p0),
                          jnp.where(lo_q, p1, 0.0), jnp.where(lo_q, 0.0, p1)], axis=0)
    s = _dot_nt(kd, qs.astype(BF16))
    yield
    s = s * (HEAD_DIM ** -0.5) - bias
    if first_valid is not None:
        s = jnp.where(lax.broadcasted_iota(jnp.int32, (band, 1), 0) >= first_valid, s, -jnp.inf)
    m = jnp.maximum(jnp.max(s, axis=0, keepdims=True), sink)
    p = jnp.exp(s - m)
    p = (p / (jnp.sum(p, axis=0, keepdims=True) + jnp.exp(sink - m))).astype(BF16)
    yield
    og = _dot_tn(p, vd)
    yield
    res[key] = jnp.concatenate([jnp.where(lo_q, og[0:q], og[q:2 * q]),
                                jnp.where(lo_q, og[2 * q:3 * q], og[3 * q:4 * q])], axis=1)


def _run_staggered(gens, stagger=True):
    live, started = [], 0
    while started < len(gens) or live:
        while started < len(gens):
            live.append(gens[started])
            started += 1
            if stagger:
                break
        for g in list(live):
            try:
                next(g)
            except StopIteration:
                live.remove(g)
        yield


def _drain(gen):
    for _ in gen:
        pass


def _state_rows(s2):
    s_t = s2.T
    half = D_SSD // 2
    return jnp.concatenate([s_t[0:half, 0:D_STATE], s_t[half:D_SSD, D_STATE:2 * D_STATE]], axis=0)


def _head_rms(x, ones_ref, g_ref):
    w = ones_ref.shape[0]
    ones = ones_ref[...]
    sq = x * x
    ssq = jnp.concatenate([_dot_exact_lhs(sq[:, i:i + w], ones, 2)
                           for i in range(0, x.shape[1], w)], axis=1)
    return x * lax.rsqrt(ssq * (1.0 / HEAD_DIM) + EPS) * g_ref[...]


def _in_proj(h, win_ref, lo, hi):
    return _dot(h, win_ref[:, lo:hi])


def _expand_head_lanes(v):
    n = v.shape[0]
    lo = _lane_lo((n, 128))
    pairs = []
    for j in range(SSD_HEADS // 2):
        even = jnp.broadcast_to(v[:, 2 * j:2 * j + 1], (n, 128))
        odd = jnp.broadcast_to(v[:, 2 * j + 1:2 * j + 2], (n, 128))
        pairs.append(jnp.where(lo, even, odd))
    return jnp.concatenate(pairs, axis=1)


def _decay_terms(h, win_ref, dtb_ref, alog_ref, tril_ref):
    dt = _softplus(_in_proj(h, win_ref, C_DT, C_Q) + dtb_ref[...])
    acs = _dot_exact_rhs(tril_ref[...], dt * (-jnp.exp(alog_ref[...])), 3)
    return _expand_head_lanes(dt), _expand_head_lanes(acs)


def _run_ordered(order, gens):
    for who in order:
        next(gens[who], None)
    for g in gens.values():
        _drain(g)


def _ada_kernel(c_ref, w_ref, b_ref, o_ref):
    c = c_ref[...]
    o_ref[0] = _dot(_silu(c).astype(BF16), w_ref[0].astype(BF16)) + b_ref[0]


def _ada_call(c_all, w_ada, b_ada):
    depth = w_ada.shape[0]
    n = c_all.shape[0]
    blk = 1024
    return pl.pallas_call(
        _ada_kernel,
        grid=(depth, 6 * D_MODEL // blk),
        in_specs=[
            pl.BlockSpec((n, D_MODEL), lambda l, j: (0, 0)),
            pl.BlockSpec((1, D_MODEL, blk), lambda l, j: (l, 0, j)),
            pl.BlockSpec((1, 1, blk), lambda l, j: (l, 0, j)),
        ],
        out_specs=pl.BlockSpec((1, n, blk), lambda l, j: (l, 0, j)),
        out_shape=jax.ShapeDtypeStruct((depth, n, 6 * D_MODEL), F32),
        compiler_params=pltpu.CompilerParams(
            dimension_semantics=("arbitrary", "arbitrary"), vmem_limit_bytes=VMEM_LIMIT),
        name="adaln",
    )(c_all, w_ada, b_ada.reshape(depth, 1, 6 * D_MODEL))


def _ffn_norm(x, sh, sc, nffn_ref):
    return (_rms(x, nffn_ref[...]) * (1.0 + sc) + sh).astype(BF16)


def _ffn_steps(put_out, x, h, g, hist_ref, wup_ref, cw_ref, cb_ref, wdn_ref, tail_ref,
               ubuf, gbuf, n_seq, length):
    rows = n_seq * length

    def up(j):
        return [_dot(h, wup_ref[:, base + j * FF_BLK:base + (j + 1) * FF_BLK]) for base in (0, D_FF)]

    acts = []
    raws = up(0)
    yield
    for j in range(N_FF_BLK):
        nxt = up(j + 1) if j + 1 < N_FF_BLK else None
        halves = []
        for raw, buf, base in zip(raws, (ubuf, gbuf), (0, D_FF)):
            lo = base + j * FF_BLK
            cw, cb = cw_ref.at[:, lo:lo + FF_BLK], cb_ref.at[:, lo:lo + FF_BLK]
            if n_seq == 1:
                y = _causal_conv_rolled(raw, hist_ref[0, :, lo:lo + FF_BLK], cw, cb, FFN_CONV)
                tail_ref[0, :, lo:lo + FF_BLK] = raw[length - HIST:length]
            else:
                buf[:, 0:HIST, :] = hist_ref[:, :, lo:lo + FF_BLK]
                y = _causal_conv(buf, raw.reshape(n_seq, length, FF_BLK), cw, cb, FFN_CONV)
                tail_ref[:, :, lo:lo + FF_BLK] = buf[:, length:length + HIST, :]
            halves.append(y.reshape(rows, FF_BLK))
        acts.append((_silu(halves[1]) * halves[0]).astype(BF16))
        raws = nxt
        yield
    k_split = (N_FF_BLK // 2 + 1) * FF_BLK
    act_a = jnp.concatenate(acts[:N_FF_BLK // 2 + 1], axis=1)
    act_b = jnp.concatenate(acts[N_FF_BLK // 2 + 1:], axis=1)
    for n in range(D_MODEL // FF_BLK):
        cols = slice(n * FF_BLK, (n + 1) * FF_BLK)
        d = _dot(act_a, wdn_ref[0:k_split, cols])
        yield
        d = d + _dot(act_b, wdn_ref[k_split:D_FF, cols])
        put_out(cols, x[:, cols] + g[:, cols] * d)
        yield


def _mixer_prompt_steps(res, x, mod, t, nmix_ref, win_ref, cw_ref, cb_ref, dtb_ref, alog_ref,
                        dsk_ref, ssdn_ref, qn_ref, kn_ref, sink_ref, bias_ref, tril_ref, onesq_ref,
                        onesk_ref, smask_ref, wout_ref, xbc_buf, kbuf, vbuf, s2_ref, tile):
    n_chunks = tile // CHUNK
    sh1, sc1, g1 = mod[0:1], mod[1:2], mod[2:3]
    h = (_rms(x, nmix_ref[...]) * (1.0 + sc1) + sh1).astype(BF16)

    xbc_raw = _in_proj(h, win_ref, C_XBC, C_DT)
    yield
    conv = _causal_conv_rolled(xbc_raw, xbc_buf[...], cw_ref, cb_ref, SSD_CONV)
    res["tail"] = xbc_raw[tile - HIST:tile]
    xbc_buf[...] = res["tail"]
    xbc = _silu(conv)
    xs, bm, cm = xbc[:, 0:D_SSD], xbc[:, D_SSD:D_SSD + D_BC], xbc[:, D_SSD + D_BC:D_XBC]
    yield
    dt, acs = _decay_terms(h, win_ref, dtb_ref, alog_ref, tril_ref)
    yield
    qn = _head_rms(_in_proj(h, win_ref, C_Q, C_K), onesq_ref, qn_ref)
    kv = _in_proj(h, win_ref, C_K, C_END)
    kn = _head_rms(kv[:, 0:D_KV], onesk_ref, kn_ref)
    vn = kv[:, D_KV:2 * D_KV]
    z = _in_proj(h, win_ref, C_Z, C_XBC)
    yield
    smask = smask_ref[...]
    dskip = dsk_ref[...]
    band = WINDOW + CHUNK
    kds = _dup_halves(jnp.concatenate([kbuf[...], kn], axis=0))
    vds = _dup_halves(jnp.concatenate([vbuf[...], vn], axis=0))
    res["k_last"] = kn[tile - WINDOW:tile]
    res["v_last"] = vn[tile - WINDOW:tile]
    kbuf[...] = res["k_last"]
    vbuf[...] = res["v_last"]
    state = {"s2": s2_ref[...]}
    part = {}
    items = []
    for c in range(n_chunks):
        r = slice(c * CHUNK, (c + 1) * CHUNK)
        kr = slice(c * CHUNK, c * CHUNK + band)
        first_valid = (WINDOW // CHUNK - (t * n_chunks + c)) * CHUNK
        items.append(_ssd_steps(part, ("y", c), acs[r], xs[r], dt[r], bm[r], cm[r], state, dskip,
                                smask, CHUNK))
        for g in range(2):
            items.append(_attn_steps(part, ("o", c, g), qn[r, 256 * g:256 * (g + 1)], kds[g][kr],
                                     vds[g][kr], bias_ref[g], sink_ref[g], first_valid, CHUNK))
    yield from _run_staggered(items)
    s2_ref[...] = state["s2"]
    y_ssd = _rms(jnp.concatenate([part["y", c] for c in range(n_chunks)], axis=0) * _silu(z),
                 ssdn_ref[...])
    o = jnp.concatenate([jnp.concatenate([part["o", c, 0], part["o", c, 1]], axis=1)
                         for c in range(n_chunks)], axis=0)
    yield
    mix = (_dot(y_ssd.astype(BF16), wout_ref[0:D_SSD, :])
           + _dot(o.astype(BF16), wout_ref[D_SSD:D_SSD + D_ATTN, :]))
    res["x_mid"] = x + g1 * mix


def _layer_prompt_kernel(x_ref, modm_ref, modf_ref,
                         nmix_ref, win_ref, cw_ref, cb_ref, dtb_ref, alog_ref, dsk_ref, ssdn_ref,
                         qn_ref, kn_ref, sink_ref, bias_ref, tril_ref, onesq_ref, onesk_ref,
                         smask_ref, wout_ref,
                         nffn_ref, wup_ref, fcw_ref, fcb_ref, wdn_ref,
                         xo_ref, ssm_ref, ctail_ref, ko_ref, vo_ref, ftail_ref,
                         xbc_buf, kbuf, vbuf, s2_ref, xmid_buf, hffn_buf, hist, *, tile, n_tiles):
    s = pl.program_id(0)
    t_mix = lax.rem(s, n_tiles)
    t_ffn = lax.rem(s + n_tiles - 1, n_tiles)
    slot_w = lax.rem(s, 2)
    slot_r = 1 - slot_w

    @pl.when(s == 0)
    def _():
        xmid_buf[...] = jnp.zeros(xmid_buf.shape, F32)
        hffn_buf[...] = jnp.zeros(hffn_buf.shape, BF16)

    @pl.when(t_mix == 0)
    def _():
        xbc_buf[...] = jnp.zeros((HIST, D_XBC), F32)
        kbuf[...] = jnp.zeros((WINDOW, D_KV), F32)
        vbuf[...] = jnp.zeros((WINDOW, D_KV), F32)
        s2_ref[...] = jnp.zeros((2 * D_STATE, D_SSD), F32)

    @pl.when(jnp.logical_or(t_ffn == 0, s == 0))
    def _():
        hist[...] = jnp.zeros(hist.shape, F32)

    mres = {}
    modm = modm_ref[0]
    mixer = _mixer_prompt_steps(
        mres, x_ref[0], modm, t_mix, nmix_ref, win_ref, cw_ref, cb_ref, dtb_ref, alog_ref,
        dsk_ref, ssdn_ref, qn_ref, kn_ref, sink_ref, bias_ref, tril_ref, onesq_ref, onesk_ref,
        smask_ref, wout_ref, xbc_buf, kbuf, vbuf, s2_ref, tile)
    def put_out(cols, v):
        xo_ref[0, :, cols] = v

    ffn = _ffn_steps(put_out, xmid_buf[slot_r], hffn_buf[slot_r], modf_ref[0, 5:6, :], hist,
                     wup_ref, fcw_ref, fcb_ref, wdn_ref, hist, None, None, 1, tile)

    def mixer_then_norm():
        yield from mixer
        yield
        xmid_buf[slot_w] = mres["x_mid"]
        hffn_buf[slot_w] = _ffn_norm(mres["x_mid"], modm[3:4], modm[4:5], nffn_ref)

    _run_ordered(STEP_ORDER, {"f": ffn, "m": mixer_then_norm()})

    @pl.when(t_mix == n_tiles - 1)
    def _():
        ssm_ref[0] = _state_rows(s2_ref[...])
        ctail_ref[0] = mres["tail"]
        ko_ref[0] = mres["k_last"]
        vo_ref[0] = mres["v_last"]

    @pl.when(jnp.logical_and(t_ffn == n_tiles - 1, s > 0))
    def _():
        ftail_ref[...] = hist[...]


def _const_spec(shape):
    zeros = (0,) * len(shape)
    return pl.BlockSpec(shape, lambda *_: zeros, pipeline_mode=pl.Buffered(1))


def _layer_spec(a, l):
    zeros = (0,) * (a.ndim - 1)
    return pl.BlockSpec((None,) + a.shape[1:], lambda *_: (l,) + zeros,
                        pipeline_mode=pl.Buffered(1))


def _layer_prompt_call(x, mod, lw, consts, l, tile):
    b, length, _ = x.shape
    n_tiles = length // tile
    n_steps = b * n_tiles
    band = WINDOW + CHUNK
    kern = functools.partial(_layer_prompt_kernel, tile=tile, n_tiles=n_tiles)

    def mix_tile(s):
        return jnp.minimum(s, n_steps - 1)

    def ffn_tile(s):
        return jnp.maximum(s - 1, 0)

    lay = lambda a: (a, _layer_spec(a, l))
    con = lambda a: (a, _const_spec(a.shape))
    params = [lay(lw["norm_mix"]), lay(lw["w_in"]), lay(lw["conv_w"]), lay(lw["conv_b"]),
              lay(lw["dt_bias"]), lay(lw["a_log"]), lay(lw["d_skip"]), lay(lw["ssd_norm"]),
              lay(lw["q_norm"]), lay(lw["k_norm"]), lay(lw["sink_p"]), con(consts["bias_p"]),
              con(consts["tril_p"]), con(consts["ones_q"]), con(consts["ones_k"]),
              con(consts["smask"]), lay(lw["w_out"]),
              lay(lw["norm_ffn"]), lay(lw["w_up"]), lay(lw["conv_ffn_w"]), lay(lw["conv_ffn_b"]),
              lay(lw["w_down"])]
    assert consts["bias_p"].shape == (2, band, 4 * CHUNK)
    return pl.pallas_call(
        kern,
        grid=(n_steps + 1,),
        in_specs=[
            pl.BlockSpec((1, tile, D_MODEL),
                         lambda s: (mix_tile(s) // n_tiles, mix_tile(s) % n_tiles, 0)),
            pl.BlockSpec((None, 1, 8, D_MODEL), lambda s: (l, mix_tile(s) // n_tiles, 0, 0)),
            pl.BlockSpec((None, 1, 8, D_MODEL), lambda s: (l, ffn_tile(s) // n_tiles, 0, 0)),
        ] + [spec for _, spec in params],
        out_specs=[
            pl.BlockSpec((1, tile, D_MODEL),
                         lambda s: (ffn_tile(s) // n_tiles, ffn_tile(s) % n_tiles, 0)),
            pl.BlockSpec((1, D_SSD, D_STATE), lambda s: (mix_tile(s) // n_tiles, 0, 0)),
            pl.BlockSpec((1, HIST, D_XBC), lambda s: (mix_tile(s) // n_tiles, 0, 0)),
            pl.BlockSpec((1, WINDOW, D_KV), lambda s: (mix_tile(s) // n_tiles, 0, 0)),
            pl.BlockSpec((1, WINDOW, D_KV), lambda s: (mix_tile(s) // n_tiles, 0, 0)),
            pl.BlockSpec((1, HIST, 2 * D_FF), lambda s: (ffn_tile(s) // n_tiles, 0, 0)),
        ],
        out_shape=[
            jax.ShapeDtypeStruct((b, length, D_MODEL), F32),
            jax.ShapeDtypeStruct((b, D_SSD, D_STATE), F32),
            jax.ShapeDtypeStruct((b, HIST, D_XBC), F32),
            jax.ShapeDtypeStruct((b, WINDOW, D_KV), F32),
            jax.ShapeDtypeStruct((b, WINDOW, D_KV), F32),
            jax.ShapeDtypeStruct((b, HIST, 2 * D_FF), F32),
        ],
        scratch_shapes=[
            pltpu.VMEM((HIST, D_XBC), F32),
            pltpu.VMEM((WINDOW, D_KV), F32),
            pltpu.VMEM((WINDOW, D_KV), F32),
            pltpu.VMEM((2 * D_STATE, D_SSD), F32),
            pltpu.VMEM((2, tile, D_MODEL), F32),
            pltpu.VMEM((2, tile, D_MODEL), BF16),
            pltpu.VMEM((1, HIST, 2 * D_FF), F32),
        ],
        compiler_params=pltpu.CompilerParams(
            dimension_semantics=("arbitrary",), vmem_limit_bytes=VMEM_LIMIT),
        name="layer_prompt",
    )(x, mod, mod, *[a for a, _ in params])


def _seq_rows(mod_ref, k, length):
    v = mod_ref[k]
    n_seq, d = v.shape
    return jnp.broadcast_to(v[:, None, :], (n_seq, length, d)).reshape(n_seq * length, d)


def _mixer_sample_kernel(x_ref, mod_ref, ssm_in_ref, ccache_ref, kc_ref, vc_ref,
                         nmix_ref, win_ref, cw_ref, cb_ref, dtb_ref, alog_ref,
                         dsk_ref, ssdn_ref, qn_ref, kn_ref, sink_ref, bias_ref, tril_ref,
                         onesq_ref, onesk_ref, smask_ref, wout_ref,
                         xo_ref, ssm_ref, ctail_ref, ko_ref, vo_ref,
                         xbc_buf, *, n_seq, length):
    rows = n_seq * length
    x = x_ref[...]
    h = (_rms(x, nmix_ref[...]) * (1.0 + _seq_rows(mod_ref, 1, length))
         + _seq_rows(mod_ref, 0, length)).astype(BF16)

    xbc_raw = _in_proj(h, win_ref, C_XBC, C_DT)
    xbc_buf[:, 0:HIST, :] = ccache_ref[...]
    conv = _causal_conv(xbc_buf, xbc_raw.reshape(n_seq, length, D_XBC), cw_ref, cb_ref, SSD_CONV)
    ctail_ref[...] = xbc_buf[:, length:length + HIST, :]
    xbc = _silu(conv.reshape(rows, D_XBC))
    xs, bm, cm = xbc[:, 0:D_SSD], xbc[:, D_SSD:D_SSD + D_BC], xbc[:, D_SSD + D_BC:D_XBC]
    dt, acs = _decay_terms(h, win_ref, dtb_ref, alog_ref, tril_ref)
    qn = _head_rms(_in_proj(h, win_ref, C_Q, C_K), onesq_ref, qn_ref)
    kv = _in_proj(h, win_ref, C_K, C_END)
    kn = _head_rms(kv[:, 0:D_KV], onesk_ref, kn_ref)
    vn = kv[:, D_KV:2 * D_KV]
    smask = smask_ref[...]
    dskip = dsk_ref[...]
    part = {}

    def ssd_item(i, r):
        state = {"s2": ssm_in_ref[i]}
        yield from _ssd_steps(part, ("y", i), acs[r], xs[r], dt[r], bm[r], cm[r], state, dskip,
                              smask, length)
        ssm_ref[i] = _state_rows(state["s2"])

    def attn_item(i, r):
        kband = jnp.concatenate([kc_ref[i], kn[r]], axis=0)
        vband = jnp.concatenate([vc_ref[i], vn[r]], axis=0)
        ko_ref[i] = kband[length:length + WINDOW]
        vo_ref[i] = vband[length:length + WINDOW]
        kds, vds = _dup_halves(kband), _dup_halves(vband)
        groups = [_attn_steps(part, ("o", i, g), qn[r, 256 * g:256 * (g + 1)], kds[g], vds[g],
                              bias_ref[g], sink_ref[g], None, length) for g in range(2)]
        yield from _run_staggered(groups, stagger=False)

    items = []
    for i in range(n_seq):
        r = slice(i * length, (i + 1) * length)
        items += [ssd_item(i, r), attn_item(i, r)]
    _drain(_run_staggered(items))
    y = jnp.concatenate([part["y", i] for i in range(n_seq)], axis=0)
    o = jnp.concatenate([jnp.concatenate([part["o", i, 0], part["o", i, 1]], axis=1)
                         for i in range(n_seq)], axis=0)

    z = _in_proj(h, win_ref, C_Z, C_XBC)
    y_ssd = _rms(y * _silu(z), ssdn_ref[...])
    mix = (_dot(y_ssd.astype(BF16), wout_ref[0:D_SSD, :])
           + _dot(o.astype(BF16), wout_ref[D_SSD:D_SSD + D_ATTN, :]))
    xo_ref[...] = x + _seq_rows(mod_ref, 2, length) * mix


def _lead_spec(shape):
    tail_zeros = (0,) * (len(shape) - 1)
    return pl.BlockSpec(shape, lambda i: (i,) + tail_zeros)


def _mod_spec(n_blk, l):
    return pl.BlockSpec((None, 6, n_blk, D_MODEL), lambda i: (l, 0, i, 0))


def _seq_spec(a, n_blk, l):
    zeros = (0,) * (a.ndim - 2)
    return pl.BlockSpec((None, n_blk) + a.shape[2:], lambda i: (l, i) + zeros)


def _mixer_sample_call(x2, mod, ssm_in, ccache, kc, vc, lw, consts, l, n_blk, length):
    n_seq = ssm_in.shape[1]
    rows = n_blk * length
    kern = functools.partial(_mixer_sample_kernel, n_seq=n_blk, length=length)
    seq_args = (ssm_in, ccache, kc, vc)
    lay = lambda a: (a, _layer_spec(a, l))
    con = lambda a: (a, _const_spec(a.shape))
    params = [lay(lw["norm_mix"]), lay(lw["w_in"]), lay(lw["conv_w"]), lay(lw["conv_b"]),
              lay(lw["dt_bias"]), lay(lw["a_log"]), lay(lw["d_skip"]), lay(lw["ssd_norm"]),
              lay(lw["q_norm"]), lay(lw["k_norm"]), lay(lw["sink_s"]), con(consts["bias_s"]),
              con(consts["tril_s"]), con(consts["ones_q"]), con(consts["ones_k"]),
              con(consts["smask"]), lay(lw["w_out"])]
    out_shape = [
        jax.ShapeDtypeStruct((n_seq * length, D_MODEL), F32),
        jax.ShapeDtypeStruct((n_seq, D_SSD, D_STATE), F32),
        jax.ShapeDtypeStruct((n_seq, HIST, D_XBC), F32),
        jax.ShapeDtypeStruct((n_seq, WINDOW, D_KV), F32),
        jax.ShapeDtypeStruct((n_seq, WINDOW, D_KV), F32),
    ]
    return pl.pallas_call(
        kern,
        grid=(n_seq // n_blk,),
        in_specs=([_lead_spec((rows, D_MODEL)), _mod_spec(n_blk, l)]
                  + [_seq_spec(a, n_blk, l) for a in seq_args]
                  + [spec for _, spec in params]),
        out_specs=([_lead_spec((rows, D_MODEL))]
                   + [_lead_spec((n_blk,) + s.shape[1:]) for s in out_shape[1:]]),
        out_shape=out_shape,
        scratch_shapes=[
            pltpu.VMEM((n_blk, HIST + length, D_XBC), F32),
        ],
        compiler_params=pltpu.CompilerParams(
            dimension_semantics=("arbitrary",), vmem_limit_bytes=VMEM_LIMIT),
        name="mixer_sample",
    )(x2, mod, *seq_args, *[a for a, _ in params])


def _ffn_sample_kernel(x_ref, mod_ref, hist_ref, nffn_ref, wup_ref, cw_ref, cb_ref,
                       wdn_ref, xo_ref, tail_ref, ubuf, gbuf, *, n_seq, length):
    x = x_ref[...]
    h = _ffn_norm(x, _seq_rows(mod_ref, 3, length), _seq_rows(mod_ref, 4, length), nffn_ref)
    def put_out(cols, v):
        xo_ref[:, cols] = v

    _drain(_ffn_steps(put_out, x, h, _seq_rows(mod_ref, 5, length), hist_ref, wup_ref, cw_ref,
                      cb_ref, wdn_ref, tail_ref, ubuf, gbuf, n_seq, length))


def _ffn_sample_call(x2, mod, hist, lw, l, n_blk, length):
    n_seq = hist.shape[1]
    rows = n_blk * length
    kern = functools.partial(_ffn_sample_kernel, n_seq=n_blk, length=length)
    const_args = (lw["norm_ffn"], lw["w_up"], lw["conv_ffn_w"], lw["conv_ffn_b"], lw["w_down"])
    out_shape = [
        jax.ShapeDtypeStruct((n_seq * length, D_MODEL), F32),
        jax.ShapeDtypeStruct((n_seq, HIST, 2 * D_FF), F32),
    ]
    return pl.pallas_call(
        kern,
        grid=(n_seq // n_blk,),
        in_specs=([_lead_spec((rows, D_MODEL)), _mod_spec(n_blk, l), _seq_spec(hist, n_blk, l)]
                  + [_layer_spec(a, l) for a in const_args]),
        out_specs=[_lead_spec((rows, D_MODEL)), _lead_spec((n_blk, HIST, 2 * D_FF))],
        out_shape=out_shape,
        scratch_shapes=[
            pltpu.VMEM((n_blk, HIST + length, FF_BLK), F32),
            pltpu.VMEM((n_blk, HIST + length, FF_BLK), F32),
        ],
        compiler_params=pltpu.CompilerParams(
            dimension_semantics=("arbitrary",), vmem_limit_bytes=VMEM_LIMIT),
        name="ffn_sample",
    )(x2, mod, hist, *const_args)


def _expand_heads(v):
    return jnp.repeat(v.astype(F32), SSD_HEAD_DIM, axis=1)[:, None, :]


def _pad_heads(v):
    return jnp.pad(v.astype(F32), ((0, 0), (0, C_Q - C_DT - SSD_HEADS)))[:, None, :]


def _prep_params(w_in, conv_ssd_w, conv_ssd_b, dt_bias, a_log, d_skip, ssd_norm, q_norm, k_norm,
                 sinks, w_out, norm_mix, norm_ffn, w_up, conv_ffn_w, conv_ffn_b, w_down, q_sample):
    o_dt = D_SSD + D_XBC
    o_q = o_dt + SSD_HEADS
    w_dt = jnp.pad(w_in[:, :, o_dt:o_q], ((0, 0), (0, 0), (0, C_Q - C_DT - SSD_HEADS)))
    w_in_r = jnp.concatenate(
        [w_in[:, :, :o_dt], w_dt, w_in[:, :, o_q:]], axis=2).astype(BF16)
    assert w_in_r.shape[2] == C_END
    return {
        "norm_mix": norm_mix[:, None, :],
        "w_in": w_in_r,
        "conv_w": conv_ssd_w,
        "conv_b": conv_ssd_b[:, None, :],
        "dt_bias": _pad_heads(dt_bias),
        "a_log": _pad_heads(a_log),
        "d_skip": _expand_heads(d_skip),
        "ssd_norm": ssd_norm[:, None, :],
        "q_norm": jnp.tile(q_norm, (1, ATTN_HEADS))[:, None, :],
        "k_norm": jnp.tile(k_norm, (1, D_KV // HEAD_DIM))[:, None, :],
        "sink_p": _sink_rows(sinks, CHUNK),
        "sink_s": _sink_rows(sinks, q_sample),
        "w_out": w_out.astype(BF16),
        "norm_ffn": norm_ffn[:, None, :],
        "w_up": w_up.astype(BF16),
        "conv_ffn_w": conv_ffn_w,
        "conv_ffn_b": conv_ffn_b[:, None, :],
        "w_down": w_down.astype(BF16),
    }


def _block_tril(n, blk):
    i = np.arange(n)
    return jnp.asarray(((i[:, None] // blk) == (i[None, :] // blk)) & (i[None, :] <= i[:, None]),
                       dtype=BF16)


def _block_ones(n, blk):
    i = np.arange(n)
    return jnp.asarray((i[:, None] // blk) == (i[None, :] // blk), dtype=BF16)


def _alibi_bias(q):
    slopes = (2.0 ** (-8.0 * np.arange(1, ATTN_HEADS + 1) / ATTN_HEADS)).astype(np.float32)
    dist = np.abs(np.arange(q)[:, None] + WINDOW - np.arange(WINDOW + q)[None, :]).astype(np.float32)
    bias = slopes[:, None, None] * dist[None]
    return jnp.asarray(np.transpose(bias.reshape(2, 4 * q, WINDOW + q), (0, 2, 1)))


def _sink_rows(sinks, q):
    return jnp.repeat(sinks.astype(F32), q, axis=1).reshape(sinks.shape[0], 2, 1, 4 * q)


def _state_mask():
    r = np.arange(2 * D_STATE)[:, None] // D_STATE
    c = np.arange(D_SSD)[None, :] // (D_SSD // 2)
    return jnp.asarray(r == c, dtype=F32)


def _state_to_s2(state):
    lead = state.shape[:2]
    st = jnp.transpose(state, (0, 1, 4, 2, 3)).reshape(lead + (D_STATE, D_SSD))
    half = D_SSD // 2
    zero = jnp.zeros(lead + (D_STATE, half), state.dtype)
    top = jnp.concatenate([st[..., :half], zero], axis=-1)
    bot = jnp.concatenate([zero, st[..., half:]], axis=-1)
    return jnp.concatenate([top, bot], axis=-2)


def _pad_hist(cache):
    k = cache.shape[2]
    return jnp.pad(cache, ((0, 0), (0, 0), (HIST - k, 0), (0, 0)))


def kernel(x_prompt, x_sample, c_prompt, c_sample, state_ssm, cache_conv_ssd, cache_attn_k,
           cache_attn_v, cache_conv_ffn, w_ada, b_ada, norm_mix, w_in, conv_ssd_w, conv_ssd_b,
           dt_bias, a_log, d_skip, ssd_norm, q_norm, k_norm, sinks, w_out, norm_ffn, w_up,
           conv_ffn_w, conv_ffn_b, w_down):
    depth = w_ada.shape[0]
    bp, seq, _ = x_prompt.shape
    bs, lsamp, _ = x_sample.shape
    rows_s = bs * lsamp
    tile = TILE_T
    n_blk = SAMPLE_SEQS

    mod_all = _ada_call(jnp.concatenate([c_prompt, c_sample], axis=0), w_ada, b_ada)
    mod_all = mod_all.reshape(depth, bp + bs, 6, D_MODEL)
    mod_p = jnp.pad(mod_all[:, :bp], ((0, 0), (0, 0), (0, 2), (0, 0)))
    mod_s = jnp.transpose(mod_all[:, bp:], (0, 2, 1, 3))

    consts = {
        "tril_p": _block_tril(tile, CHUNK),
        "tril_s": _block_tril(n_blk * lsamp, lsamp),
        "ones_q": _block_ones(MXU_TILE, HEAD_DIM),
        "ones_k": _block_ones(D_KV, HEAD_DIM),
        "smask": _state_mask(),
        "bias_p": _alibi_bias(CHUNK),
        "bias_s": _alibi_bias(lsamp),
    }

    lw = _prep_params(w_in, conv_ssd_w, conv_ssd_b, dt_bias, a_log, d_skip, ssd_norm, q_norm, k_norm,
                      sinks, w_out, norm_mix, norm_ffn, w_up, conv_ffn_w, conv_ffn_b, w_down, lsamp)
    ssm_in = _state_to_s2(state_ssm)
    conv_in = _pad_hist(cache_conv_ssd)
    ffn_in = _pad_hist(cache_conv_ffn)
    k_in = cache_attn_k.reshape(depth, bs, WINDOW, D_KV)
    v_in = cache_attn_v.reshape(depth, bs, WINDOW, D_KV)

    xp = x_prompt
    xs = x_sample.reshape(rows_s, D_MODEL)
    outs = [[] for _ in range(10)]
    for l in range(depth):
        xp, ssm_p, ct_p, k_p, v_p, ft_p = _layer_prompt_call(xp, mod_p, lw, consts, l, tile)
        xs, ssm_s, ct_s, k_s, v_s = _mixer_sample_call(
            xs, mod_s, ssm_in, conv_in, k_in, v_in, lw, consts, l, n_blk, lsamp)
        xs, ft_s = _ffn_sample_call(xs, mod_s, ffn_in, lw, l, n_blk, lsamp)

        kv_shape = (WINDOW, D_KV // HEAD_DIM, HEAD_DIM)
        for lst, val in zip(outs, (
                ssm_p.reshape(bp, SSD_HEADS, SSD_HEAD_DIM, D_STATE),
                ssm_s.reshape(bs, SSD_HEADS, SSD_HEAD_DIM, D_STATE),
                ct_p[:, HIST - (SSD_CONV - 1):], ct_s[:, HIST - (SSD_CONV - 1):],
                k_p.reshape((bp,) + kv_shape), k_s.reshape((bs,) + kv_shape),
                v_p.reshape((bp,) + kv_shape), v_s.reshape((bs,) + kv_shape),
                ft_p[:, HIST - (FFN_CONV - 1):], ft_s[:, HIST - (FFN_CONV - 1):])):
            lst.append(val)

    return (xp, xs.reshape(bs, lsamp, D_MODEL)) + tuple(jnp.stack(o) for o in outs)
```
